```python
import jax, jax.numpy as jnp
from jax import lax
import numpy as np

D_MODEL = 1024
BATCH = 8
SEQ = 8192
DEPTH = 2

CHUNK = 64
N_MIXERS = 2
POOL_WINDOWS = (2, 4, 8, 16)
N_POOL_GROUPS = len(POOL_WINDOWS)
POOL_GROUP_DIM = D_MODEL // N_POOL_GROUPS
CONV_WIDTH = 3
D_FF = ((8 * D_MODEL // 3 + 255) // 256) * 256
N_SUB = 3
N_MOD = 3
RMS_EPS = 1e-6
N_POOL_LAYERS = (DEPTH + 1) // 2
N_CONV_LAYERS = DEPTH // 2

kernel_name = "hybrid_pool_shortconv_macaron_adaln"


def rms_norm(x, g):
    xf = x.astype(jnp.float32)
    y = xf * lax.rsqrt(jnp.mean(xf * xf, axis=-1, keepdims=True) + RMS_EPS)
    return (y * g.astype(jnp.float32)).astype(x.dtype)


def modulate(h, shift, scale):
    return h * (1 + scale[:, None, :]) + shift[:, None, :]


def swiglu(h, w_in, w_out):
    a, b = jnp.split(h @ w_in, 2, axis=-1)
    return (jax.nn.silu(a) * b) @ w_out


def pool_mixer(h, w_grp, scale):
    B, T, D = h.shape
    hg = h.reshape(B, T, N_POOL_GROUPS, POOL_GROUP_DIM)
    cs = jnp.cumsum(hg.astype(jnp.float32), axis=1)
    pos = jnp.arange(1, T + 1, dtype=jnp.float32)
    pooled = []
    for g, w in enumerate(POOL_WINDOWS):
        csg = cs[:, :, g]
        lag = jnp.pad(csg, ((0, 0), (w, 0), (0, 0)))[:, :T]
        cnt = jnp.minimum(pos, float(w))[None, :, None]
        pooled.append((csg - lag) / cnt)
    pooled = jnp.stack(pooled, axis=2).astype(h.dtype)
    mixed = pooled - hg
    y = jnp.einsum('btgc,gcd->btgd', mixed, w_grp).reshape(B, T, D)
    return y * scale


def conv_mixer(h, w_in, w_conv, w_out):
    D = h.shape[-1]
    b_gate, c_gate, v = jnp.split(h @ w_in, 3, axis=-1)
    u = c_gate * v
    conv = lax.conv_general_dilated(
        u, w_conv[:, None, :].astype(u.dtype),
        window_strides=(1,),
        padding=[(CONV_WIDTH - 1, 0)],
        dimension_numbers=('NWC', 'WIO', 'NWC'),
        feature_group_count=D)
    return (b_gate * conv) @ w_out


def setup_inputs(seed: int = 0) -> dict:
    key = jax.random.key(seed)
    ks = jax.random.split(key, 14)
    f32 = jnp.float32
    x = jax.random.normal(ks[0], (BATCH, SEQ, D_MODEL), f32)
    c = jax.random.normal(ks[1], (BATCH, D_MODEL), f32)
    ada_w = jax.random.normal(ks[2], (DEPTH, D_MODEL, N_SUB * N_MOD * D_MODEL), f32) * D_MODEL ** -0.5
    ada_b = 0.01 * jax.random.normal(ks[3], (DEPTH, N_SUB * N_MOD * D_MODEL), f32)
    norm_g = 1.0 + 0.1 * jax.random.normal(ks[4], (DEPTH, N_SUB, D_MODEL), f32)
    ffn_w_in = jax.random.normal(ks[5], (DEPTH, 2, D_MODEL, 2 * D_FF), f32) * D_MODEL ** -0.5
    ffn_w_out = jax.random.normal(ks[6], (DEPTH, 2, D_FF, D_MODEL), f32) * D_FF ** -0.5
    pool_w = jax.random.normal(ks[7], (N_POOL_LAYERS, N_POOL_GROUPS, POOL_GROUP_DIM, POOL_GROUP_DIM), f32) * POOL_GROUP_DIM ** -0.5
    pool_scale = 1.0 + 0.1 * jax.random.normal(ks[8], (N_POOL_LAYERS, D_MODEL), f32)
    conv_w_in = jax.random.normal(ks[9], (N_CONV_LAYERS, D_MODEL, 3 * D_MODEL), f32) * D_MODEL ** -0.5
    conv_w = jax.random.normal(ks[10], (N_CONV_LAYERS, CONV_WIDTH, D_MODEL), f32) * CONV_WIDTH ** -0.5
    conv_w_out = jax.random.normal(ks[11], (N_CONV_LAYERS, D_MODEL, D_MODEL), f32) * D_MODEL ** -0.5
    final_g = 1.0 + 0.1 * jax.random.normal(ks[12], (D_MODEL,), f32)
    return {"x": x, "c": c, "ada_w": ada_w, "ada_b": ada_b, "norm_g": norm_g,
            "ffn_w_in": ffn_w_in, "ffn_w_out": ffn_w_out,
            "pool_w": pool_w, "pool_scale": pool_scale,
            "conv_w_in": conv_w_in, "conv_w": conv_w, "conv_w_out": conv_w_out,
            "final_g": final_g}


def reference(x, c, ada_w, ada_b, norm_g, ffn_w_in, ffn_w_out, pool_w, pool_scale,
              conv_w_in, conv_w, conv_w_out, final_g):
    B = x.shape[0]
    cond = jax.nn.silu(c)
    for i in range(DEPTH):
        mod = (cond @ ada_w[i] + ada_b[i]).reshape(B, N_SUB, N_MOD, D_MODEL)
        shift, scale, gate = mod[:, :, 0], mod[:, :, 1], mod[:, :, 2]

        h = modulate(rms_norm(x, norm_g[i, 0]), shift[:, 0], scale[:, 0])
        x = x + 0.5 * gate[:, 0, None, :] * swiglu(h, ffn_w_in[i, 0], ffn_w_out[i, 0])

        h = modulate(rms_norm(x, norm_g[i, 1]), shift[:, 1], scale[:, 1])
        j = i // N_MIXERS
        if i % N_MIXERS == 0:
            y = pool_mixer(h, pool_w[j], pool_scale[j])
        else:
            y = conv_mixer(h, conv_w_in[j], conv_w[j], conv_w_out[j])
        x = x + gate[:, 1, None, :] * y

        h = modulate(rms_norm(x, norm_g[i, 2]), shift[:, 2], scale[:, 2])
        x = x + 0.5 * gate[:, 2, None, :] * swiglu(h, ffn_w_in[i, 1], ffn_w_out[i, 1])
    return rms_norm(x, final_g)
```

```python
import functools

import jax
import jax.numpy as jnp
from jax import lax
from jax.experimental import pallas as pl
from jax.experimental.pallas import tpu as pltpu

D_MODEL = 1024
DEPTH = 2
N_MIXERS = 2
POOL_WINDOWS = (2, 4, 8, 16)
N_POOL_GROUPS = len(POOL_WINDOWS)
POOL_GROUP_DIM = D_MODEL // N_POOL_GROUPS
CONV_WIDTH = 3
D_FF = ((8 * D_MODEL // 3 + 255) // 256) * 256
N_SUB = 3
N_MOD = 3
RMS_EPS = 1e-6

V7X_MXU_COLS = 256
V7X_VMEM_BYTES = 64 * 1024 * 1024
BF16_SUBLANES = 16

FF_CHUNK = V7X_MXU_COLS
HALO = BF16_SUBLANES
VMEM_LIMIT = V7X_VMEM_BYTES - 8 * 1024 * 1024

F32 = jnp.float32
BF16 = jnp.bfloat16


def _norm_mod(x, g, shift, scale):
    ms = jnp.mean(x * x, axis=-1, keepdims=True)
    y = x * lax.rsqrt(ms + RMS_EPS)
    return (y * g) * (1.0 + scale) + shift


def _mod_rows(mod_ref, sub):
    r = N_MOD * sub
    return (mod_ref[0, 0, r:r + 1, :], mod_ref[0, 0, r + 1:r + 2, :],
            mod_ref[0, 0, r + 2:r + 3, :])


def _adaln_kernel(c_ref, w_ref, b_ref, o_ref):
    c = c_ref[...]
    cond = c * jax.nn.sigmoid(c)
    o_ref[0] = jnp.dot(cond, w_ref[0], preferred_element_type=F32,
                       precision=lax.Precision.HIGHEST) + b_ref[0]


def _adaln(c, ada_w, ada_b):
    n_out = N_SUB * N_MOD * D_MODEL
    tn = 1536
    b, d = c.shape
    return pl.pallas_call(
        _adaln_kernel,
        grid=(DEPTH, n_out // tn),
        in_specs=[
            pl.BlockSpec((b, d), lambda i, n: (0, 0)),
            pl.BlockSpec((1, d, tn), lambda i, n: (i, 0, n)),
            pl.BlockSpec((1, 1, tn), lambda i, n: (i, 0, n)),
        ],
        out_specs=pl.BlockSpec((1, b, tn), lambda i, n: (i, 0, n)),
        out_shape=jax.ShapeDtypeStruct((DEPTH, b, n_out), F32),
        compiler_params=pltpu.CompilerParams(
            dimension_semantics=("parallel", "parallel"), vmem_limit_bytes=VMEM_LIMIT),
        name="adaln",
    )(c, ada_w, ada_b.reshape(DEPTH, 1, n_out))


def _ffn_kernel(x_ref, mod_ref, g_ref, win_ref, wout_ref, *rest, sub, g_row, final):
    if final:
        fg_ref, o_ref, act_ref = rest
    else:
        o_ref, act_ref = rest
    x = x_ref[0]
    shift, scale, gate = _mod_rows(mod_ref, sub)
    hb = _norm_mod(x, g_ref[g_row:g_row + 1, :], shift, scale).astype(BF16)
    for c in range(D_FF // FF_CHUNK):
        lo = c * FF_CHUNK
        a = jnp.dot(hb, win_ref[0, 0, :, lo:lo + FF_CHUNK], preferred_element_type=F32)
        b = jnp.dot(hb, win_ref[0, 0, :, D_FF + lo:D_FF + lo + FF_CHUNK],
                    preferred_element_type=F32)
        act_ref[:, lo:lo + FF_CHUNK] = (a * jax.nn.sigmoid(a) * b).astype(BF16)
    y = jnp.dot(act_ref[...], wout_ref[0, 0], preferred_element_type=F32)
    out = x + (0.5 * gate) * y
    if final:
        ms = jnp.mean(out * out, axis=-1, keepdims=True)
        out = (out * lax.rsqrt(ms + RMS_EPS)) * fg_ref[...]
    o_ref[0] = out


def _ffn(x, mod, norm_g, w_in, w_out, *, layer, which, sub, tm, final_g=None):
    bsz, t, d = x.shape
    final = final_g is not None
    kern = functools.partial(_ffn_kernel, sub=sub, g_row=N_SUB * layer + sub, final=final)
    resident = pl.Buffered(1)
    in_specs = [
        pl.BlockSpec((1, tm, d), lambda b, j: (b, j, 0)),
        pl.BlockSpec((1, 1, N_SUB * N_MOD, d), lambda b, j: (layer, b, 0, 0)),
        pl.BlockSpec(norm_g.shape, lambda b, j: (0, 0)),
        pl.BlockSpec((1, 1, d, 2 * D_FF), lambda b, j: (layer, which, 0, 0),
                     pipeline_mode=resident),
        pl.BlockSpec((1, 1, D_FF, d), lambda b, j: (layer, which, 0, 0),
                     pipeline_mode=resident),
    ]
    args = [x, mod, norm_g, w_in, w_out]
    if final:
        in_specs.append(pl.BlockSpec((1, d), lambda b, j: (0, 0)))
        args.append(final_g.reshape(1, d))
    return pl.pallas_call(
        kern,
        grid=(bsz, t // tm),
        in_specs=in_specs,
        out_specs=pl.BlockSpec((1, tm, d), lambda b, j: (b, j, 0)),
        out_shape=jax.ShapeDtypeStruct(x.shape, x.dtype),
        scratch_shapes=[pltpu.VMEM((tm, D_FF), BF16)],
        compiler_params=pltpu.CompilerParams(
            dimension_semantics=("parallel", "parallel"), vmem_limit_bytes=VMEM_LIMIT),
        name=f"ffn_l{layer}_{which}",
    )(*args)


def _halo_spec(tm, d):
    per_tile = tm // HALO
    return pl.BlockSpec((1, HALO, d), lambda b, j: (b, jnp.maximum(j * per_tile - 1, 0), 0))


def _pool_kernel(x_ref, halo_ref, mod_ref, g_ref, pw_ref, ps_ref, o_ref, *, sub, g_row, tm):
    j = pl.program_id(1)
    x = x_ref[0]
    shift, scale, gate = _mod_rows(mod_ref, sub)
    g = g_ref[g_row:g_row + 1, :]
    h = _norm_mod(x, g, shift, scale)
    hh = _norm_mod(halo_ref[0], g, shift, scale)
    hh = jnp.where(j > 0, hh, 0.0)
    pos = j * tm + 1 + lax.broadcasted_iota(jnp.int32, (tm, 1), 0)
    ys = []
    for grp, w in enumerate(POOL_WINDOWS):
        lo = grp * POOL_GROUP_DIM
        hg = h[:, lo:lo + POOL_GROUP_DIM]
        s = jnp.concatenate([hh[:, lo:lo + POOL_GROUP_DIM], hg], axis=0)
        k = 1
        while k < w:
            s = s + pltpu.roll(s, k, 0)
            k *= 2
        inv_cnt = 1.0 / jnp.minimum(pos, w).astype(F32)
        mixed = s[HALO:] * inv_cnt - hg
        ys.append(jnp.dot(mixed.astype(BF16), pw_ref[0, grp], preferred_element_type=F32))
    y = jnp.concatenate(ys, axis=-1) * ps_ref[...]
    o_ref[0] = x + gate * y


def _pool(x, mod, norm_g, pool_w, pool_scale, *, layer, j_pool, sub, tm):
    bsz, t, d = x.shape
    kern = functools.partial(_pool_kernel, sub=sub, g_row=N_SUB * layer + sub, tm=tm)
    return pl.pallas_call(
        kern,
        grid=(bsz, t // tm),
        in_specs=[
            pl.BlockSpec((1, tm, d), lambda b, j: (b, j, 0)),
            _halo_spec(tm, d),
            pl.BlockSpec((1, 1, N_SUB * N_MOD, d), lambda b, j: (layer, b, 0, 0)),
            pl.BlockSpec(norm_g.shape, lambda b, j: (0, 0)),
            pl.BlockSpec((1, N_POOL_GROUPS, POOL_GROUP_DIM, POOL_GROUP_DIM),
                         lambda b, j: (j_pool, 0, 0, 0)),
            pl.BlockSpec((1, d), lambda b, j: (j_pool, 0)),
        ],
        out_specs=pl.BlockSpec((1, tm, d), lambda b, j: (b, j, 0)),
        out_shape=jax.ShapeDtypeStruct(x.shape, x.dtype),
        compiler_params=pltpu.CompilerParams(
            dimension_semantics=("parallel", "parallel"), vmem_limit_bytes=VMEM_LIMIT),
        name=f"pool_l{layer}",
    )(x, x, mod, norm_g, pool_w, pool_scale)


def _conv_kernel(x_ref, halo_ref, mod_ref, g_ref, win_ref, cw_ref, wout_ref, o_ref,
                 *, sub, g_row, tm):
    j = pl.program_id(1)
    d = D_MODEL
    x = x_ref[0]
    shift, scale, gate = _mod_rows(mod_ref, sub)
    g = g_ref[g_row:g_row + 1, :]
    hb = _norm_mod(x, g, shift, scale).astype(BF16)
    hhb = _norm_mod(halo_ref[0], g, shift, scale).astype(BF16)
    ext = jnp.concatenate([hhb, hb], axis=0)
    b_gate = jnp.dot(hb, win_ref[0, :, :d], preferred_element_type=F32)
    cv = jnp.dot(ext, win_ref[0, :, d:], preferred_element_type=F32)
    u = cv[:, :d] * cv[:, d:]
    row = lax.broadcasted_iota(jnp.int32, (HALO + tm, 1), 0)
    u = jnp.where(jnp.logical_or(row >= HALO, j > 0), u, 0.0)
    conv = (cw_ref[0, 2:3, :] * u[HALO:]
            + cw_ref[0, 1:2, :] * pltpu.roll(u, 1, 0)[HALO:]
            + cw_ref[0, 0:1, :] * pltpu.roll(u, 2, 0)[HALO:])
    z = (b_gate * conv).astype(BF16)
    y = jnp.dot(z, wout_ref[0], preferred_element_type=F32)
    o_ref[0] = x + gate * y


def _conv(x, mod, norm_g, w_in, w_conv, w_out, *, layer, j_conv, sub, tm):
    bsz, t, d = x.shape
    kern = functools.partial(_conv_kernel, sub=sub, g_row=N_SUB * layer + sub, tm=tm)
    resident = pl.Buffered(1)
    return pl.pallas_call(
        kern,
        grid=(bsz, t // tm),
        in_specs=[
            pl.BlockSpec((1, tm, d), lambda b, j: (b, j, 0)),
            _halo_spec(tm, d),
            pl.BlockSpec((1, 1, N_SUB * N_MOD, d), lambda b, j: (layer, b, 0, 0)),
            pl.BlockSpec(norm_g.shape, lambda b, j: (0, 0)),
            pl.BlockSpec((1, d, 3 * d), lambda b, j: (j_conv, 0, 0), pipeline_mode=resident),
            pl.BlockSpec((1, CONV_WIDTH, d), lambda b, j: (j_conv, 0, 0)),
            pl.BlockSpec((1, d, d), lambda b, j: (j_conv, 0, 0), pipeline_mode=resident),
        ],
        out_specs=pl.BlockSpec((1, tm, d), lambda b, j: (b, j, 0)),
        out_shape=jax.ShapeDtypeStruct(x.shape, x.dtype),
        compiler_params=pltpu.CompilerParams(
            dimension_semantics=("parallel", "parallel"), vmem_limit_bytes=VMEM_LIMIT),
        name=f"conv_l{layer}",
    )(x, x, mod, norm_g, w_in, w_conv, w_out)


def kernel(x, c, ada_w, ada_b, norm_g, ffn_w_in, ffn_w_out, pool_w, pool_scale,
           conv_w_in, conv_w, conv_w_out, final_g):
    bsz, t, d = x.shape
    tm = 512
    mod = _adaln(c, ada_w, ada_b).reshape(DEPTH, bsz, N_SUB * N_MOD, d)
    norm_g2 = norm_g.reshape(DEPTH * N_SUB, d)
    ffn_w_in_b = ffn_w_in.astype(BF16)
    ffn_w_out_b = ffn_w_out.astype(BF16)
    pool_w_b = pool_w.astype(BF16)
    conv_w_in_b = conv_w_in.astype(BF16)
    conv_w_out_b = conv_w_out.astype(BF16)
    for i in range(DEPTH):
        x = _ffn(x, mod, norm_g2, ffn_w_in_b, ffn_w_out_b, layer=i, which=0, sub=0, tm=tm)
        j = i // N_MIXERS
        if i % N_MIXERS == 0:
            x = _pool(x, mod, norm_g2, pool_w_b, pool_scale, layer=i, j_pool=j, sub=1, tm=tm)
        else:
            x = _conv(x, mod, norm_g2, conv_w_in_b, conv_w, conv_w_out_b,
                      layer=i, j_conv=j, sub=1, tm=tm)
        x = _ffn(x, mod, norm_g2, ffn_w_in_b, ffn_w_out_b, layer=i, which=1, sub=2, tm=tm,
                 final_g=final_g if i == DEPTH - 1 else None)
    return x
```

```python
import functools

import jax
import jax.numpy as jnp
from jax import lax
from jax.experimental import pallas as pl
from jax.experimental.pallas import tpu as pltpu

D_MODEL = 1024
DEPTH = 2
N_MIXERS = 2
POOL_WINDOWS = (2, 4, 8, 16)
N_POOL_GROUPS = len(POOL_WINDOWS)
POOL_GROUP_DIM = D_MODEL // N_POOL_GROUPS
CONV_WIDTH = 3
D_FF = ((8 * D_MODEL // 3 + 255) // 256) * 256
N_SUB = 3
N_MOD = 3
RMS_EPS = 1e-6

V7X_MXU_COLS = 256
V7X_VMEM_BYTES = 64 * 1024 * 1024
F32_SUBLANES = 8
BF16_SUBLANES = 16

FF_CHUNK = V7X_MXU_COLS
POOL_HALO = max(POOL_WINDOWS)
CONV_HALO = F32_SUBLANES
VMEM_LIMIT = V7X_VMEM_BYTES - 8 * 1024 * 1024

F32 = jnp.float32
BF16 = jnp.bfloat16


def _norm_mod(x, gm, shift):
    ms = jnp.mean(x * x, axis=-1, keepdims=True)
    return (x * lax.rsqrt(ms + RMS_EPS)) * gm + shift


def _mod_rows(mod_ref, g_ref, g_row, sub):
    r = N_MOD * sub
    shift = mod_ref[0, 0, r:r + 1, :]
    scale = mod_ref[0, 0, r + 1:r + 2, :]
    gate = mod_ref[0, 0, r + 2:r + 3, :]
    return g_ref[g_row:g_row + 1, :] * (1.0 + scale), shift, gate


def _adaln_kernel(c_ref, w_ref, b_ref, o_ref):
    c = c_ref[...]
    cond = c * jax.nn.sigmoid(c)
    o_ref[0] = jnp.dot(cond, w_ref[0], preferred_element_type=F32,
                       precision=lax.Precision.HIGHEST) + b_ref[0]


def _adaln(c, ada_w, ada_b):
    n_out = N_SUB * N_MOD * D_MODEL
    tn = 1536
    b, d = c.shape
    return pl.pallas_call(
        _adaln_kernel,
        grid=(DEPTH, n_out // tn),
        in_specs=[
            pl.BlockSpec((b, d), lambda i, n: (0, 0)),
            pl.BlockSpec((1, d, tn), lambda i, n: (i, 0, n)),
            pl.BlockSpec((1, 1, tn), lambda i, n: (i, 0, n)),
        ],
        out_specs=pl.BlockSpec((1, b, tn), lambda i, n: (i, 0, n)),
        out_shape=jax.ShapeDtypeStruct((DEPTH, b, n_out), F32),
        compiler_params=pltpu.CompilerParams(
            dimension_semantics=("parallel", "parallel"), vmem_limit_bytes=VMEM_LIMIT),
        name="adaln",
    )(c, ada_w, ada_b.reshape(DEPTH, 1, n_out))


def _ffn_kernel(x_ref, mod_ref, g_ref, win_ref, wout_ref, *rest, sub, g_row, final, ts):
    if final:
        fg_ref, o_ref, act_ref = rest
    else:
        o_ref, act_ref = rest
    gm, shift, gate = _mod_rows(mod_ref, g_ref, g_row, sub)
    half_gate = 0.5 * gate
    for s in range(x_ref.shape[1] // ts):
        rows = pl.ds(s * ts, ts)
        x = x_ref[0, rows, :]
        hb = _norm_mod(x, gm, shift).astype(BF16)
        for c in range(D_FF // FF_CHUNK):
            lo = c * FF_CHUNK
            a = jnp.dot(hb, win_ref[0, 0, :, lo:lo + FF_CHUNK], preferred_element_type=F32)
            b = jnp.dot(hb, win_ref[0, 0, :, D_FF + lo:D_FF + lo + FF_CHUNK],
                        preferred_element_type=F32)
            act_ref[s % 2, :, lo:lo + FF_CHUNK] = (a * jax.nn.sigmoid(a) * b).astype(BF16)
        y = jnp.dot(act_ref[s % 2], wout_ref[0, 0], preferred_element_type=F32)
        out = x + half_gate * y
        if final:
            ms = jnp.mean(out * out, axis=-1, keepdims=True)
            out = (out * lax.rsqrt(ms + RMS_EPS)) * fg_ref[...]
        o_ref[0, rows, :] = out


def _ffn(x, mod, norm_g, w_in, w_out, *, layer, which, sub, tm, ts, final_g=None):
    bsz, t, d = x.shape
    final = final_g is not None
    kern = functools.partial(_ffn_kernel, sub=sub, g_row=N_SUB * layer + sub, final=final,
                             ts=ts)
    resident = pl.Buffered(1)
    in_specs = [
        pl.BlockSpec((1, tm, d), lambda b, j: (b, j, 0)),
        pl.BlockSpec((1, 1, N_SUB * N_MOD, d), lambda b, j: (layer, b, 0, 0)),
        pl.BlockSpec(norm_g.shape, lambda b, j: (0, 0)),
        pl.BlockSpec((1, 1, d, 2 * D_FF), lambda b, j: (layer, which, 0, 0),
                     pipeline_mode=resident),
        pl.BlockSpec((1, 1, D_FF, d), lambda b, j: (layer, which, 0, 0),
                     pipeline_mode=resident),
    ]
    args = [x, mod, norm_g, w_in, w_out]
    if final:
        in_specs.append(pl.BlockSpec((1, d), lambda b, j: (0, 0)))
        args.append(final_g.reshape(1, d))
    return pl.pallas_call(
        kern,
        grid=(bsz, t // tm),
        in_specs=in_specs,
        out_specs=pl.BlockSpec((1, tm, d), lambda b, j: (b, j, 0)),
        out_shape=jax.ShapeDtypeStruct(x.shape, x.dtype),
        scratch_shapes=[pltpu.VMEM((2, ts, D_FF), BF16)],
        compiler_params=pltpu.CompilerParams(
            dimension_semantics=("parallel", "parallel"), vmem_limit_bytes=VMEM_LIMIT),
        name=f"ffn_l{layer}_{which}",
    )(*args)


def _pool_kernel(x_ref, halo_ref, mod_ref, g_ref, pw_ref, ps_ref, o_ref, *, sub, g_row, ts):
    j = pl.program_id(1)
    tm = x_ref.shape[1]
    gm, shift, gate = _mod_rows(mod_ref, g_ref, g_row, sub)
    out_scale = ps_ref[...] * gate
    hh = jnp.where(j > 0, _norm_mod(halo_ref[0], gm, shift), 0.0)
    for s in range(tm // ts):
        rows = pl.ds(s * ts, ts)
        x = x_ref[0, rows, :]
        h = _norm_mod(x, gm, shift)
        pos = j * tm + s * ts + 1 + lax.broadcasted_iota(jnp.int32, (ts, 1), 0)
        ys = []
        for grp, w in enumerate(POOL_WINDOWS):
            lo = grp * POOL_GROUP_DIM
            hg = h[:, lo:lo + POOL_GROUP_DIM]
            acc = jnp.concatenate([hh[:, lo:lo + POOL_GROUP_DIM], hg], axis=0)
            k = 1
            while k < w:
                acc = acc + pltpu.roll(acc, k, 0)
                k *= 2
            inv_cnt = 1.0 / jnp.minimum(pos, w).astype(F32)
            mixed = acc[POOL_HALO:] * inv_cnt - hg
            ys.append(jnp.dot(mixed.astype(BF16), pw_ref[0, grp], preferred_element_type=F32))
        o_ref[0, rows, :] = x + jnp.concatenate(ys, axis=-1) * out_scale
        hh = h[ts - POOL_HALO:]


def _pool(x, mod, norm_g, pool_w, pool_scale, *, layer, j_pool, sub, tm, ts):
    bsz, t, d = x.shape
    kern = functools.partial(_pool_kernel, sub=sub, g_row=N_SUB * layer + sub, ts=ts)
    per_tile = tm // POOL_HALO
    return pl.pallas_call(
        kern,
        grid=(bsz, t // tm),
        in_specs=[
            pl.BlockSpec((1, tm, d), lambda b, j: (b, j, 0)),
            pl.BlockSpec((1, POOL_HALO, d),
                         lambda b, j: (b, jnp.maximum(j * per_tile - 1, 0), 0)),
            pl.BlockSpec((1, 1, N_SUB * N_MOD, d), lambda b, j: (layer, b, 0, 0)),
            pl.BlockSpec(norm_g.shape, lambda b, j: (0, 0)),
            pl.BlockSpec((1, N_POOL_GROUPS, POOL_GROUP_DIM, POOL_GROUP_DIM),
                         lambda b, j: (j_pool, 0, 0, 0)),
            pl.BlockSpec((1, d), lambda b, j: (j_pool, 0)),
        ],
        out_specs=pl.BlockSpec((1, tm, d), lambda b, j: (b, j, 0)),
        out_shape=jax.ShapeDtypeStruct(x.shape, x.dtype),
        compiler_params=pltpu.CompilerParams(
            dimension_semantics=("parallel", "parallel"), vmem_limit_bytes=VMEM_LIMIT),
        name=f"pool_l{layer}",
    )(x, x, mod, norm_g, pool_w, pool_scale)


def _conv_kernel(x_ref, mod_ref, g_ref, win_ref, cw_ref, wout_ref, o_ref, tail_ref,
                 *, sub, g_row, ts):
    j = pl.program_id(1)
    d = D_MODEL
    gm, shift, gate = _mod_rows(mod_ref, g_ref, g_row, sub)
    @pl.when(j == 0)
    def _():
        tail_ref[...] = jnp.zeros_like(tail_ref)

    tail = tail_ref[...]
    for s in range(x_ref.shape[1] // ts):
        rows = pl.ds(s * ts, ts)
        x = x_ref[0, rows, :]
        hb = _norm_mod(x, gm, shift).astype(BF16)
        bcv = jnp.dot(hb, win_ref[0], preferred_element_type=F32)
        u = bcv[:, d:2 * d] * bcv[:, 2 * d:]
        ext = jnp.concatenate([tail, u], axis=0)
        conv = (cw_ref[0, 2:3, :] * u
                + cw_ref[0, 1:2, :] * pltpu.roll(ext, 1, 0)[CONV_HALO:]
                + cw_ref[0, 0:1, :] * pltpu.roll(ext, 2, 0)[CONV_HALO:])
        z = (bcv[:, :d] * conv).astype(BF16)
        y = jnp.dot(z, wout_ref[0], preferred_element_type=F32)
        o_ref[0, rows, :] = x + gate * y
        tail = u[ts - CONV_HALO:]
    tail_ref[...] = tail


def _conv(x, mod, norm_g, w_in, w_conv, w_out, *, layer, j_conv, sub, tm, ts):
    bsz, t, d = x.shape
    kern = functools.partial(_conv_kernel, sub=sub, g_row=N_SUB * layer + sub, ts=ts)
    resident = pl.Buffered(1)
    return pl.pallas_call(
        kern,
        grid=(bsz, t // tm),
        in_specs=[
            pl.BlockSpec((1, tm, d), lambda b, j: (b, j, 0)),
            pl.BlockSpec((1, 1, N_SUB * N_MOD, d), lambda b, j: (layer, b, 0, 0)),
            pl.BlockSpec(norm_g.shape, lambda b, j: (0, 0)),
            pl.BlockSpec((1, d, 3 * d), lambda b, j: (j_conv, 0, 0), pipeline_mode=resident),
            pl.BlockSpec((1, CONV_WIDTH, d), lambda b, j: (j_conv, 0, 0)),
            pl.BlockSpec((1, d, d), lambda b, j: (j_conv, 0, 0), pipeline_mode=resident),
        ],
        out_specs=pl.BlockSpec((1, tm, d), lambda b, j: (b, j, 0)),
        out_shape=jax.ShapeDtypeStruct(x.shape, x.dtype),
        scratch_shapes=[pltpu.VMEM((CONV_HALO, d), F32)],
        compiler_params=pltpu.CompilerParams(
            dimension_semantics=("arbitrary", "arbitrary"), vmem_limit_bytes=VMEM_LIMIT),
        name=f"conv_l{layer}",
    )(x, mod, norm_g, w_in, w_conv, w_out)


def kernel(x, c, ada_w, ada_b, norm_g, ffn_w_in, ffn_w_out, pool_w, pool_scale,
           conv_w_in, conv_w, conv_w_out, final_g):
    bsz, t, d = x.shape
    tm, ts = 1024, 256
    mod = _adaln(c, ada_w, ada_b).reshape(DEPTH, bsz, N_SUB * N_MOD, d)
    norm_g2 = norm_g.reshape(DEPTH * N_SUB, d)
    ffn_w_in_b = ffn_w_in.astype(BF16)
    ffn_w_out_b = ffn_w_out.astype(BF16)
    pool_w_b = pool_w.astype(BF16)
    conv_w_in_b = conv_w_in.astype(BF16)
    conv_w_out_b = conv_w_out.astype(BF16)
    for i in range(DEPTH):
        x = _ffn(x, mod, norm_g2, ffn_w_in_b, ffn_w_out_b, layer=i, which=0, sub=0,
                 tm=tm, ts=ts)
        j = i // N_MIXERS
        if i % N_MIXERS == 0:
            x = _pool(x, mod, norm_g2, pool_w_b, pool_scale, layer=i, j_pool=j, sub=1,
                      tm=tm, ts=ts)
        else:
            x = _conv(x, mod, norm_g2, conv_w_in_b, conv_w, conv_w_out_b,
                      layer=i, j_conv=j, sub=1, tm=tm, ts=ts)
        x = _ffn(x, mod, norm_g2, ffn_w_in_b, ffn_w_out_b, layer=i, which=1, sub=2,
                 tm=tm, ts=ts, final_g=final_g if i == DEPTH - 1 else None)
    return x
```

```python
import functools

import jax
import jax.numpy as jnp
from jax import lax
from jax.experimental import pallas as pl
from jax.experimental.pallas import tpu as pltpu

D_MODEL = 1024
DEPTH = 2
N_MIXERS = 2
POOL_WINDOWS = (2, 4, 8, 16)
N_POOL_GROUPS = len(POOL_WINDOWS)
POOL_GROUP_DIM = D_MODEL // N_POOL_GROUPS
CONV_WIDTH = 3
D_FF = ((8 * D_MODEL // 3 + 255) // 256) * 256
N_SUB = 3
N_MOD = 3
RMS_EPS = 1e-6

V7X_MXU_COLS = 256
V7X_VMEM_BYTES = 64 * 1024 * 1024
F32_SUBLANES = 8
BF16_SUBLANES = 16

FF_CHUNK = V7X_MXU_COLS
CONV_CHUNK = V7X_MXU_COLS
POOL_HALO = max(POOL_WINDOWS)
CONV_HALO = F32_SUBLANES
VMEM_LIMIT = V7X_VMEM_BYTES - 8 * 1024 * 1024

F32 = jnp.float32
BF16 = jnp.bfloat16


def _norm_mod(x, gm, shift):
    ms = jnp.mean(x * x, axis=-1, keepdims=True)
    return (x * lax.rsqrt(ms + RMS_EPS)) * gm + shift


def _mod_rows(mod_ref, g_ref, g_row, sub):
    r = N_MOD * sub
    shift = mod_ref[0, 0, r:r + 1, :]
    scale = mod_ref[0, 0, r + 1:r + 2, :]
    gate = mod_ref[0, 0, r + 2:r + 3, :]
    return g_ref[g_row:g_row + 1, :] * (1.0 + scale), shift, gate


def _adaln_kernel(c_ref, w_ref, b_ref, o_ref):
    c = c_ref[...]
    cond = c * jax.nn.sigmoid(c)
    c_hi = cond.astype(BF16)
    c_lo = (cond - c_hi.astype(F32)).astype(BF16)
    w = w_ref[0]
    w_hi = w.astype(BF16)
    w_lo = (w - w_hi.astype(F32)).astype(BF16)
    n = c.shape[0]
    hh = jnp.dot(jnp.concatenate([c_hi, c_lo], axis=0), w_hi, preferred_element_type=F32)
    hl = jnp.dot(c_hi, w_lo, preferred_element_type=F32)
    o_ref[0] = (hh[:n] + hh[n:]) + hl + b_ref[0]


def _adaln(c, ada_w, ada_b):
    n_out = N_SUB * N_MOD * D_MODEL
    tn = 1536
    b, d = c.shape
    return pl.pallas_call(
        _adaln_kernel,
        grid=(DEPTH, n_out // tn),
        in_specs=[
            pl.BlockSpec((b, d), lambda i, n: (0, 0)),
            pl.BlockSpec((1, d, tn), lambda i, n: (i, 0, n)),
            pl.BlockSpec((1, 1, tn), lambda i, n: (i, 0, n)),
        ],
        out_specs=pl.BlockSpec((1, b, tn), lambda i, n: (i, 0, n)),
        out_shape=jax.ShapeDtypeStruct((DEPTH, b, n_out), F32),
        compiler_params=pltpu.CompilerParams(
            dimension_semantics=("parallel", "parallel"), vmem_limit_bytes=VMEM_LIMIT),
        name="adaln",
    )(c, ada_w, ada_b.reshape(DEPTH, 1, n_out))


def _ffn_kernel(*refs, sub, g_row, final, ts, n_cast):
    x_ref, mod_ref, g_ref, win_ref, wout_ref = refs[:5]
    rest = list(refs[5:])
    fg_ref = rest.pop(0) if final else None
    cast_src, rest = rest[:n_cast], rest[n_cast:]
    o_ref, rest = rest[0], rest[1:]
    cast_dst, (act_ref,) = rest[:n_cast], rest[n_cast:]

    for src, dst in zip(cast_src, cast_dst):
        dst[...] = src[...].astype(dst.dtype)

    gm, shift, gate = _mod_rows(mod_ref, g_ref, g_row, sub)
    half_gate = 0.5 * gate
    for s in range(x_ref.shape[1] // ts):
        rows = pl.ds(s * ts, ts)
        x = x_ref[0, rows, :]
        hb = _norm_mod(x, gm, shift).astype(BF16)
        for c in range(D_FF // FF_CHUNK):
            lo = c * FF_CHUNK
            a = jnp.dot(hb, win_ref[:, lo:lo + FF_CHUNK], preferred_element_type=F32)
            b = jnp.dot(hb, win_ref[:, D_FF + lo:D_FF + lo + FF_CHUNK],
                        preferred_element_type=F32)
            act_ref[s % 2, :, lo:lo + FF_CHUNK] = (a * jax.nn.sigmoid(a) * b).astype(BF16)
        y = jnp.dot(act_ref[s % 2], wout_ref[...], preferred_element_type=F32)
        out = x + half_gate * y
        if final:
            ms = jnp.mean(out * out, axis=-1, keepdims=True)
            out = (out * lax.rsqrt(ms + RMS_EPS)) * fg_ref[...]
        o_ref[0, rows, :] = out


def _cast_job(src, lead, n_steps, nj):
    r, c = src.shape[-2:]
    rows = next(k for k in range(BF16_SUBLANES, r + 1, BF16_SUBLANES)
                if r % k == 0 and r // k <= n_steps)
    last = r // rows - 1

    def block(b, j):
        return jnp.minimum(b * nj + j, last)

    src_spec = pl.BlockSpec((None,) * len(lead) + (rows, c),
                            lambda b, j: (*lead, block(b, j), 0))
    dst_spec = pl.BlockSpec((rows, c), lambda b, j: (block(b, j), 0))
    return src_spec, dst_spec, jax.ShapeDtypeStruct((r, c), BF16)


def _ffn(x, mod, norm_g, w_in, w_out, *, layer, sub, tm, ts, final_g=None, cast=()):
    bsz, t, d = x.shape
    nj = t // tm
    final = final_g is not None
    kern = functools.partial(_ffn_kernel, sub=sub, g_row=N_SUB * layer + sub, final=final,
                             ts=ts, n_cast=len(cast))
    resident = pl.Buffered(1)
    x_spec = pl.BlockSpec((1, tm, d), lambda b, j: (b, j, 0))
    in_specs = [
        x_spec,
        pl.BlockSpec((1, 1, N_SUB * N_MOD, d), lambda b, j: (layer, b, 0, 0)),
        pl.BlockSpec(norm_g.shape, lambda b, j: (0, 0)),
        pl.BlockSpec(w_in.shape, lambda b, j: (0, 0), pipeline_mode=resident),
        pl.BlockSpec(w_out.shape, lambda b, j: (0, 0), pipeline_mode=resident),
    ]
    args = [x, mod, norm_g, w_in, w_out]
    if final:
        in_specs.append(pl.BlockSpec((1, d), lambda b, j: (0, 0)))
        args.append(final_g.reshape(1, d))
    out_specs = [x_spec]
    out_shape = [jax.ShapeDtypeStruct(x.shape, x.dtype)]
    for src, lead in cast:
        src_spec, dst_spec, dst_shape = _cast_job(src, lead, bsz * nj, nj)
        in_specs.append(src_spec)
        args.append(src)
        out_specs.append(dst_spec)
        out_shape.append(dst_shape)
    outs = pl.pallas_call(
        kern,
        grid=(bsz, nj),
        in_specs=in_specs,
        out_specs=out_specs,
        out_shape=out_shape,
        scratch_shapes=[pltpu.VMEM((2, ts, D_FF), BF16)],
        compiler_params=pltpu.CompilerParams(
            dimension_semantics=("arbitrary", "arbitrary"), vmem_limit_bytes=VMEM_LIMIT),
        name=f"ffn_l{layer}_s{sub}",
    )(*args)
    return outs[0], outs[1:]


def _pool_kernel(x_ref, halo_ref, mod_ref, g_ref, pw_ref, ps_ref, o_ref, *, sub, g_row, ts):
    j = pl.program_id(1)
    tm = x_ref.shape[1]
    gm, shift, gate = _mod_rows(mod_ref, g_ref, g_row, sub)
    out_scale = ps_ref[...] * gate
    hh = jnp.where(j > 0, _norm_mod(halo_ref[0], gm, shift), 0.0)
    head_seen = j * tm + 1 + lax.broadcasted_iota(jnp.int32, (POOL_HALO, 1), 0)
    for s in range(tm // ts):
        rows = pl.ds(s * ts, ts)
        x = x_ref[0, rows, :]
        h = _norm_mod(x, gm, shift)
        ys = []
        for grp, w in enumerate(POOL_WINDOWS):
            lo = grp * POOL_GROUP_DIM
            hg = h[:, lo:lo + POOL_GROUP_DIM]
            acc = jnp.concatenate([hh[:, lo:lo + POOL_GROUP_DIM], hg], axis=0)
            k = 1
            while k < w:
                acc = acc + pltpu.roll(acc, k, 0)
                k *= 2
            win_sum = acc[POOL_HALO:]
            if s == 0:
                inv_cnt = 1.0 / jnp.minimum(head_seen, w).astype(F32)
                pooled = jnp.concatenate(
                    [win_sum[:POOL_HALO] * inv_cnt, win_sum[POOL_HALO:] * (1.0 / w)], axis=0)
            else:
                pooled = win_sum * (1.0 / w)
            mixed = pooled - hg
            ys.append(jnp.dot(mixed.astype(BF16), pw_ref[grp], preferred_element_type=F32))
        o_ref[0, rows, :] = x + jnp.concatenate(ys, axis=-1) * out_scale
        hh = h[ts - POOL_HALO:]


def _pool(x, mod, norm_g, pool_w, pool_scale, *, layer, j_pool, sub, tm, ts):
    bsz, t, d = x.shape
    kern = functools.partial(_pool_kernel, sub=sub, g_row=N_SUB * layer + sub, ts=ts)
    per_tile = tm // POOL_HALO
    return pl.pallas_call(
        kern,
        grid=(bsz, t // tm),
        in_specs=[
            pl.BlockSpec((1, tm, d), lambda b, j: (b, j, 0)),
            pl.BlockSpec((1, POOL_HALO, d),
                         lambda b, j: (b, jnp.maximum(j * per_tile - 1, 0), 0)),
            pl.BlockSpec((1, 1, N_SUB * N_MOD, d), lambda b, j: (layer, b, 0, 0)),
            pl.BlockSpec(norm_g.shape, lambda b, j: (0, 0)),
            pl.BlockSpec(pool_w.shape, lambda b, j: (0, 0, 0)),
            pl.BlockSpec((1, d), lambda b, j: (j_pool, 0)),
        ],
        out_specs=pl.BlockSpec((1, tm, d), lambda b, j: (b, j, 0)),
        out_shape=jax.ShapeDtypeStruct(x.shape, x.dtype),
        compiler_params=pltpu.CompilerParams(
            dimension_semantics=("parallel", "parallel"), vmem_limit_bytes=VMEM_LIMIT),
        name=f"pool_l{layer}",
    )(x, x, mod, norm_g, pool_w, pool_scale)


def _conv_kernel(x_ref, mod_ref, g_ref, win_ref, cw_ref, wout_ref, o_ref,
                 tail_ref, hb_ref, conv_ref, z_ref, *, sub, g_row, ts):
    j = pl.program_id(1)
    d = D_MODEL
    n_sub = x_ref.shape[1] // ts
    n_col = d // CONV_CHUNK
    gm, shift, gate = _mod_rows(mod_ref, g_ref, g_row, sub)

    @pl.when(j == 0)
    def _():
        tail_ref[...] = jnp.zeros_like(tail_ref)

    def stage_a(s):
        hb = _norm_mod(x_ref[0, pl.ds(s * ts, ts), :], gm, shift).astype(BF16)
        hb_ref[s % 2] = hb
        for n in range(n_col):
            lo = n * CONV_CHUNK
            cg = jnp.dot(hb, win_ref[:, d + lo:d + lo + CONV_CHUNK],
                         preferred_element_type=F32)
            v = jnp.dot(hb, win_ref[:, 2 * d + lo:2 * d + lo + CONV_CHUNK],
                        preferred_element_type=F32)
            u = cg * v
            ext = jnp.concatenate([tail_ref[:, lo:lo + CONV_CHUNK], u], axis=0)
            conv_ref[s % 2, :, lo:lo + CONV_CHUNK] = (
                cw_ref[0, 2:3, lo:lo + CONV_CHUNK] * u
                + cw_ref[0, 1:2, lo:lo + CONV_CHUNK] * pltpu.roll(ext, 1, 0)[CONV_HALO:]
                + cw_ref[0, 0:1, lo:lo + CONV_CHUNK] * pltpu.roll(ext, 2, 0)[CONV_HALO:])
            tail_ref[:, lo:lo + CONV_CHUNK] = u[ts - CONV_HALO:]

    def stage_b(s):
        hb = hb_ref[s % 2]
        for n in range(n_col):
            lo = n * CONV_CHUNK
            bg = jnp.dot(hb, win_ref[:, lo:lo + CONV_CHUNK], preferred_element_type=F32)
            z_ref[s % 2, :, lo:lo + CONV_CHUNK] = (
                bg * conv_ref[s % 2, :, lo:lo + CONV_CHUNK]).astype(BF16)

    def stage_c(s):
        rows = pl.ds(s * ts, ts)
        y = jnp.dot(z_ref[s % 2], wout_ref[...], preferred_element_type=F32)
        o_ref[0, rows, :] = x_ref[0, rows, :] + gate * y

    stage_a(0)
    stage_b(0)
    for s in range(1, n_sub):
        stage_a(s)
        stage_c(s - 1)
        stage_b(s)
    stage_c(n_sub - 1)


def _conv(x, mod, norm_g, w_in, w_conv, w_out, *, layer, j_conv, sub, tm, ts):
    bsz, t, d = x.shape
    kern = functools.partial(_conv_kernel, sub=sub, g_row=N_SUB * layer + sub, ts=ts)
    resident = pl.Buffered(1)
    return pl.pallas_call(
        kern,
        grid=(bsz, t // tm),
        in_specs=[
            pl.BlockSpec((1, tm, d), lambda b, j: (b, j, 0)),
            pl.BlockSpec((1, 1, N_SUB * N_MOD, d), lambda b, j: (layer, b, 0, 0)),
            pl.BlockSpec(norm_g.shape, lambda b, j: (0, 0)),
            pl.BlockSpec(w_in.shape, lambda b, j: (0, 0), pipeline_mode=resident),
            pl.BlockSpec((1, CONV_WIDTH, d), lambda b, j: (j_conv, 0, 0)),
            pl.BlockSpec(w_out.shape, lambda b, j: (0, 0), pipeline_mode=resident),
        ],
        out_specs=pl.BlockSpec((1, tm, d), lambda b, j: (b, j, 0)),
        out_shape=jax.ShapeDtypeStruct(x.shape, x.dtype),
        scratch_shapes=[pltpu.VMEM((CONV_HALO, d), F32), pltpu.VMEM((2, ts, d), BF16),
                        pltpu.VMEM((2, ts, d), F32), pltpu.VMEM((2, ts, d), BF16)],
        compiler_params=pltpu.CompilerParams(
            dimension_semantics=("arbitrary", "arbitrary"), vmem_limit_bytes=VMEM_LIMIT),
        name=f"conv_l{layer}",
    )(x, mod, norm_g, w_in, w_conv, w_out)


def kernel(x, c, ada_w, ada_b, norm_g, ffn_w_in, ffn_w_out, pool_w, pool_scale,
           conv_w_in, conv_w, conv_w_out, final_g):
    bsz, t, d = x.shape
    tm, ts = 1024, 256
    mod = _adaln(c, ada_w, ada_b).reshape(DEPTH, bsz, N_SUB * N_MOD, d)
    norm_g2 = norm_g.reshape(DEPTH * N_SUB, d)
    pool_w2 = pool_w.reshape(pool_w.shape[0], d, POOL_GROUP_DIM)
    w_in, w_out = ffn_w_in[0, 0].astype(BF16), ffn_w_out[0, 0].astype(BF16)
    for i in range(DEPTH):
        jm = i // N_MIXERS
        is_pool = i % N_MIXERS == 0
        cast = [(ffn_w_in, (i, 1)), (ffn_w_out, (i, 1))]
        cast += [(pool_w2, (jm,))] if is_pool else [(conv_w_in, (jm,)), (conv_w_out, (jm,))]
        x, (w_in, w_out, *mixer_w) = _ffn(x, mod, norm_g2, w_in, w_out, layer=i, sub=0,
                                          tm=tm, ts=ts, cast=cast)
        if is_pool:
            pool_w_b = mixer_w[0].reshape(N_POOL_GROUPS, POOL_GROUP_DIM, POOL_GROUP_DIM)
            x = _pool(x, mod, norm_g2, pool_w_b, pool_scale, layer=i, j_pool=jm, sub=1,
                      tm=tm, ts=ts)
        else:
            x = _conv(x, mod, norm_g2, mixer_w[0], conv_w, mixer_w[1],
                      layer=i, j_conv=jm, sub=1, tm=tm, ts=ts)
        last = i == DEPTH - 1
        cast = [] if last else [(ffn_w_in, (i + 1, 0)), (ffn_w_out, (i + 1, 0))]
        x, nxt = _ffn(x, mod, norm_g2, w_in, w_out, layer=i, sub=2, tm=tm, ts=ts,
                      final_g=final_g if last else None, cast=cast)
        if not last:
            w_in, w_out = nxt
    return x
```

```python
import functools

import jax
import jax.numpy as jnp
from jax import lax
from jax.experimental import pallas as pl
from jax.experimental.pallas import tpu as pltpu

D_MODEL = 1024
DEPTH = 2
N_MIXERS = 2
POOL_WINDOWS = (2, 4, 8, 16)
N_POOL_GROUPS = len(POOL_WINDOWS)
POOL_GROUP_DIM = D_MODEL // N_POOL_GROUPS
CONV_WIDTH = 3
D_FF = ((8 * D_MODEL // 3 + 255) // 256) * 256
N_SUB = 3
N_MOD = 3
RMS_EPS = 1e-6

V7X_MXU_COLS = 256
V7X_VMEM_BYTES = 64 * 1024 * 1024
F32_SUBLANES = 8
BF16_SUBLANES = 16

FF_CHUNK = V7X_MXU_COLS
CONV_CHUNK = V7X_MXU_COLS
POOL_HALO = max(POOL_WINDOWS)
CONV_HALO = F32_SUBLANES
VMEM_LIMIT = V7X_VMEM_BYTES - 8 * 1024 * 1024

F32 = jnp.float32
BF16 = jnp.bfloat16


def _norm_mod(x, gm, shift):
    ms = jnp.mean(x * x, axis=-1, keepdims=True)
    return (x * lax.rsqrt(ms + RMS_EPS)) * gm + shift


def _mod_rows(mod_ref, g_ref, g_row, sub):
    r = N_MOD * sub
    shift = mod_ref[0, 0, r:r + 1, :]
    scale = mod_ref[0, 0, r + 1:r + 2, :]
    gate = mod_ref[0, 0, r + 2:r + 3, :]
    return g_ref[g_row:g_row + 1, :] * (1.0 + scale), shift, gate


def _adaln_kernel(c_ref, w_ref, b_ref, o_ref):
    c = c_ref[...]
    cond = c * jax.nn.sigmoid(c)
    c_hi = cond.astype(BF16)
    c_lo = (cond - c_hi.astype(F32)).astype(BF16)
    w = w_ref[0]
    w_hi = w.astype(BF16)
    w_lo = (w - w_hi.astype(F32)).astype(BF16)
    n = c.shape[0]
    hh = jnp.dot(jnp.concatenate([c_hi, c_lo], axis=0), w_hi, preferred_element_type=F32)
    hl = jnp.dot(c_hi, w_lo, preferred_element_type=F32)
    o_ref[0] = (hh[:n] + hh[n:]) + hl + b_ref[0]


def _adaln(c, ada_w, ada_b):
    n_out = N_SUB * N_MOD * D_MODEL
    tn = 1536
    b, d = c.shape
    return pl.pallas_call(
        _adaln_kernel,
        grid=(DEPTH, n_out // tn),
        in_specs=[
            pl.BlockSpec((b, d), lambda i, n: (0, 0)),
            pl.BlockSpec((1, d, tn), lambda i, n: (i, 0, n)),
            pl.BlockSpec((1, 1, tn), lambda i, n: (i, 0, n)),
        ],
        out_specs=pl.BlockSpec((1, b, tn), lambda i, n: (i, 0, n)),
        out_shape=jax.ShapeDtypeStruct((DEPTH, b, n_out), F32),
        compiler_params=pltpu.CompilerParams(
            dimension_semantics=("parallel", "parallel"), vmem_limit_bytes=VMEM_LIMIT),
        name="adaln",
    )(c, ada_w, ada_b.reshape(DEPTH, 1, n_out))


def _pool_mix(h, hh, pw_ref, head_seen, first):
    ys = []
    for grp, w in enumerate(POOL_WINDOWS):
        lo = grp * POOL_GROUP_DIM
        hg = h[:, lo:lo + POOL_GROUP_DIM]
        acc = jnp.concatenate([hh[:, lo:lo + POOL_GROUP_DIM], hg], axis=0)
        k = 1
        while k < w:
            acc = acc + pltpu.roll(acc, k, 0)
            k *= 2
        win_sum = acc[POOL_HALO:]
        if first:
            inv_cnt = 1.0 / jnp.minimum(head_seen, w).astype(F32)
            pooled = jnp.concatenate(
                [win_sum[:POOL_HALO] * inv_cnt, win_sum[POOL_HALO:] * (1.0 / w)], axis=0)
        else:
            pooled = win_sum * (1.0 / w)
        mixed = pooled - hg
        ys.append(jnp.dot(mixed.astype(BF16), pw_ref[grp], preferred_element_type=F32))
    return jnp.concatenate(ys, axis=-1)


def _ffn_kernel(*refs, sub, g_row, final, pool, ts, n_cast):
    x_ref, mod_ref, g_ref, win_ref, wout_ref = refs[:5]
    rest = list(refs[5:])
    fg_ref = rest.pop(0) if final else None
    pw_ref, ps_ref = (rest.pop(0), rest.pop(0)) if pool else (None, None)
    cast_src, rest = rest[:n_cast], rest[n_cast:]
    o_ref, rest = rest[0], rest[1:]
    cast_dst, rest = rest[:n_cast], rest[n_cast:]
    act_ref = rest.pop(0)
    ptail_ref = rest.pop(0) if pool else None

    for src, dst in zip(cast_src, cast_dst):
        dst[...] = src[...].astype(dst.dtype)

    j = pl.program_id(1)
    tm = x_ref.shape[1]
    n_sub = tm // ts
    gm, shift, gate = _mod_rows(mod_ref, g_ref, g_row, sub)
    half_gate = 0.5 * gate
    if pool:
        p_gm, p_shift, p_gate = _mod_rows(mod_ref, g_ref, g_row + 1, sub + 1)
        p_scale = ps_ref[...] * p_gate
        head_seen = j * tm + 1 + lax.broadcasted_iota(jnp.int32, (POOL_HALO, 1), 0)

        @pl.when(j == 0)
        def _():
            ptail_ref[...] = jnp.zeros_like(ptail_ref)

    def hidden(s):
        hb = _norm_mod(x_ref[0, pl.ds(s * ts, ts), :], gm, shift).astype(BF16)
        for c in range(D_FF // FF_CHUNK):
            lo = c * FF_CHUNK
            a = jnp.dot(hb, win_ref[:, lo:lo + FF_CHUNK], preferred_element_type=F32)
            b = jnp.dot(hb, win_ref[:, D_FF + lo:D_FF + lo + FF_CHUNK],
                        preferred_element_type=F32)
            act_ref[s % 2, :, lo:lo + FF_CHUNK] = (a * jax.nn.sigmoid(a) * b).astype(BF16)

    def project(s):
        y = jnp.dot(act_ref[s % 2], wout_ref[...], preferred_element_type=F32)
        out = x_ref[0, pl.ds(s * ts, ts), :] + half_gate * y
        if final:
            ms = jnp.mean(out * out, axis=-1, keepdims=True)
            out = (out * lax.rsqrt(ms + RMS_EPS)) * fg_ref[...]
        return out

    def mix(s, x1):
        h = _norm_mod(x1, p_gm, p_shift)
        y = _pool_mix(h, ptail_ref[...], pw_ref, head_seen, first=(s == 0))
        ptail_ref[...] = h[ts - POOL_HALO:]
        return x1 + y * p_scale

    if not pool:
        for s in range(n_sub):
            hidden(s)
            o_ref[0, pl.ds(s * ts, ts), :] = project(s)
    else:
        hidden(0)
        x1 = project(0)
        for s in range(1, n_sub):
            hidden(s)
            o_ref[0, pl.ds((s - 1) * ts, ts), :] = mix(s - 1, x1)
            x1 = project(s)
        o_ref[0, pl.ds((n_sub - 1) * ts, ts), :] = mix(n_sub - 1, x1)


def _cast_job(src, lead, n_steps, nj):
    r, c = src.shape[-2:]
    rows = next(k for k in range(BF16_SUBLANES, r + 1, BF16_SUBLANES)
                if r % k == 0 and r // k <= n_steps)
    last = r // rows - 1

    def block(b, j):
        return jnp.minimum(b * nj + j, last)

    src_spec = pl.BlockSpec((None,) * len(lead) + (rows, c),
                            lambda b, j: (*lead, block(b, j), 0))
    dst_spec = pl.BlockSpec((rows, c), lambda b, j: (block(b, j), 0))
    return src_spec, dst_spec, jax.ShapeDtypeStruct((r, c), BF16)


def _ffn(x, mod, norm_g, w_in, w_out, *, layer, sub, tm, ts, final_g=None, pool=None,
         cast=()):
    bsz, t, d = x.shape
    nj = t // tm
    final = final_g is not None
    kern = functools.partial(_ffn_kernel, sub=sub, g_row=N_SUB * layer + sub, final=final,
                             pool=pool is not None, ts=ts, n_cast=len(cast))
    resident = pl.Buffered(1)
    x_spec = pl.BlockSpec((1, tm, d), lambda b, j: (b, j, 0))
    in_specs = [
        x_spec,
        pl.BlockSpec((1, 1, N_SUB * N_MOD, d), lambda b, j: (layer, b, 0, 0)),
        pl.BlockSpec(norm_g.shape, lambda b, j: (0, 0)),
        pl.BlockSpec(w_in.shape, lambda b, j: (0, 0), pipeline_mode=resident),
        pl.BlockSpec(w_out.shape, lambda b, j: (0, 0), pipeline_mode=resident),
    ]
    args = [x, mod, norm_g, w_in, w_out]
    if final:
        in_specs.append(pl.BlockSpec((1, d), lambda b, j: (0, 0)))
        args.append(final_g.reshape(1, d))
    scratch = [pltpu.VMEM((2, ts, D_FF), BF16)]
    if pool is not None:
        pool_w, pool_scale, j_pool = pool
        in_specs += [pl.BlockSpec(pool_w.shape, lambda b, j: (0, 0, 0)),
                     pl.BlockSpec((1, d), lambda b, j: (j_pool, 0))]
        args += [pool_w, pool_scale]
        scratch.append(pltpu.VMEM((POOL_HALO, d), F32))
    out_specs = [x_spec]
    out_shape = [jax.ShapeDtypeStruct(x.shape, x.dtype)]
    for src, lead in cast:
        src_spec, dst_spec, dst_shape = _cast_job(src, lead, bsz * nj, nj)
        in_specs.append(src_spec)
        args.append(src)
        out_specs.append(dst_spec)
        out_shape.append(dst_shape)
    outs = pl.pallas_call(
        kern,
        grid=(bsz, nj),
        in_specs=in_specs,
        out_specs=out_specs,
        out_shape=out_shape,
        scratch_shapes=scratch,
        compiler_params=pltpu.CompilerParams(
            dimension_semantics=("arbitrary", "arbitrary"), vmem_limit_bytes=VMEM_LIMIT),
        name=f"ffn_l{layer}_s{sub}",
    )(*args)
    return outs[0], outs[1:]


def _conv_kernel(x_ref, mod_ref, g_ref, win_ref, cw_ref, wout_ref, o_ref,
                 tail_ref, hb_ref, conv_ref, z_ref, *, sub, g_row, ts):
    j = pl.program_id(1)
    d = D_MODEL
    n_sub = x_ref.shape[1] // ts
    n_col = d // CONV_CHUNK
    gm, shift, gate = _mod_rows(mod_ref, g_ref, g_row, sub)

    @pl.when(j == 0)
    def _():
        tail_ref[...] = jnp.zeros_like(tail_ref)

    def stage_a(s):
        hb = _norm_mod(x_ref[0, pl.ds(s * ts, ts), :], gm, shift).astype(BF16)
        hb_ref[s % 2] = hb
        for n in range(n_col):
            lo = n * CONV_CHUNK
            cg = jnp.dot(hb, win_ref[:, d + lo:d + lo + CONV_CHUNK],
                         preferred_element_type=F32)
            v = jnp.dot(hb, win_ref[:, 2 * d + lo:2 * d + lo + CONV_CHUNK],
                        preferred_element_type=F32)
            u = cg * v
            ext = jnp.concatenate([tail_ref[:, lo:lo + CONV_CHUNK], u], axis=0)
            conv_ref[s % 2, :, lo:lo + CONV_CHUNK] = (
                cw_ref[0, 2:3, lo:lo + CONV_CHUNK] * u
                + cw_ref[0, 1:2, lo:lo + CONV_CHUNK] * pltpu.roll(ext, 1, 0)[CONV_HALO:]
                + cw_ref[0, 0:1, lo:lo + CONV_CHUNK] * pltpu.roll(ext, 2, 0)[CONV_HALO:])
            tail_ref[:, lo:lo + CONV_CHUNK] = u[ts - CONV_HALO:]

    def stage_b(s):
        hb = hb_ref[s % 2]
        for n in range(n_col):
            lo = n * CONV_CHUNK
            bg = jnp.dot(hb, win_ref[:, lo:lo + CONV_CHUNK], preferred_element_type=F32)
            z_ref[s % 2, :, lo:lo + CONV_CHUNK] = (
                bg * conv_ref[s % 2, :, lo:lo + CONV_CHUNK]).astype(BF16)

    def stage_c(s):
        rows = pl.ds(s * ts, ts)
        y = jnp.dot(z_ref[s % 2], wout_ref[...], preferred_element_type=F32)
        o_ref[0, rows, :] = x_ref[0, rows, :] + gate * y

    stage_a(0)
    stage_b(0)
    for s in range(1, n_sub):
        stage_a(s)
        stage_c(s - 1)
        stage_b(s)
    stage_c(n_sub - 1)


def _conv(x, mod, norm_g, w_in, w_conv, w_out, *, layer, j_conv, sub, tm, ts):
    bsz, t, d = x.shape
    kern = functools.partial(_conv_kernel, sub=sub, g_row=N_SUB * layer + sub, ts=ts)
    resident = pl.Buffered(1)
    return pl.pallas_call(
        kern,
        grid=(bsz, t // tm),
        in_specs=[
            pl.BlockSpec((1, tm, d), lambda b, j: (b, j, 0)),
            pl.BlockSpec((1, 1, N_SUB * N_MOD, d), lambda b, j: (layer, b, 0, 0)),
            pl.BlockSpec(norm_g.shape, lambda b, j: (0, 0)),
            pl.BlockSpec(w_in.shape, lambda b, j: (0, 0), pipeline_mode=resident),
            pl.BlockSpec((1, CONV_WIDTH, d), lambda b, j: (j_conv, 0, 0)),
            pl.BlockSpec(w_out.shape, lambda b, j: (0, 0), pipeline_mode=resident),
        ],
        out_specs=pl.BlockSpec((1, tm, d), lambda b, j: (b, j, 0)),
        out_shape=jax.ShapeDtypeStruct(x.shape, x.dtype),
        scratch_shapes=[pltpu.VMEM((CONV_HALO, d), F32), pltpu.VMEM((2, ts, d), BF16),
                        pltpu.VMEM((2, ts, d), F32), pltpu.VMEM((2, ts, d), BF16)],
        compiler_params=pltpu.CompilerParams(
            dimension_semantics=("arbitrary", "arbitrary"), vmem_limit_bytes=VMEM_LIMIT),
        name=f"conv_l{layer}",
    )(x, mod, norm_g, w_in, w_conv, w_out)


def kernel(x, c, ada_w, ada_b, norm_g, ffn_w_in, ffn_w_out, pool_w, pool_scale,
           conv_w_in, conv_w, conv_w_out, final_g):
    bsz, t, d = x.shape
    tm, ts = 1024, 256
    mod = _adaln(c, ada_w, ada_b).reshape(DEPTH, bsz, N_SUB * N_MOD, d)
    norm_g2 = norm_g.reshape(DEPTH * N_SUB, d)
    w_in, w_out = ffn_w_in[0, 0].astype(BF16), ffn_w_out[0, 0].astype(BF16)
    pool_w_b = pool_w.astype(BF16)
    for i in range(DEPTH):
        jm = i // N_MIXERS
        cast = [(ffn_w_in, (i, 1)), (ffn_w_out, (i, 1))]
        if i % N_MIXERS == 0:
            x, (w_in, w_out) = _ffn(x, mod, norm_g2, w_in, w_out, layer=i, sub=0, tm=tm, ts=ts,
                                    pool=(pool_w_b[jm], pool_scale, jm), cast=cast)
        else:
            cast += [(conv_w_in, (jm,)), (conv_w_out, (jm,))]
            x, (w_in, w_out, cw_in, cw_out) = _ffn(x, mod, norm_g2, w_in, w_out, layer=i,
                                                   sub=0, tm=tm, ts=ts, cast=cast)
            x = _conv(x, mod, norm_g2, cw_in, conv_w, cw_out,
                      layer=i, j_conv=jm, sub=1, tm=tm, ts=ts)
        last = i == DEPTH - 1
        cast = [] if last else [(ffn_w_in, (i + 1, 0)), (ffn_w_out, (i + 1, 0))]
        x, nxt = _ffn(x, mod, norm_g2, w_in, w_out, layer=i, sub=2, tm=tm, ts=ts,
                      final_g=final_g if last else None, cast=cast)
        if not last:
            w_in, w_out = nxt
    return x
```

```python
import functools

import jax
import jax.numpy as jnp
from jax import lax
from jax.experimental import pallas as pl
from jax.experimental.pallas import tpu as pltpu

D_MODEL = 1024
DEPTH = 2
N_MIXERS = 2
POOL_WINDOWS = (2, 4, 8, 16)
N_POOL_GROUPS = len(POOL_WINDOWS)
POOL_GROUP_DIM = D_MODEL // N_POOL_GROUPS
CONV_WIDTH = 3
D_FF = ((8 * D_MODEL // 3 + 255) // 256) * 256
N_SUB = 3
N_MOD = 3
RMS_EPS = 1e-6

V7X_MXU_COLS = 256
V7X_VMEM_BYTES = 64 * 1024 * 1024
F32_SUBLANES = 8
BF16_SUBLANES = 16

FF_CHUNK = V7X_MXU_COLS
CONV_CHUNK = V7X_MXU_COLS
POOL_HALO = max(POOL_WINDOWS)
CONV_HALO = F32_SUBLANES
VMEM_LIMIT = V7X_VMEM_BYTES - 4 * 1024 * 1024

F32 = jnp.float32
BF16 = jnp.bfloat16


def _norm_mod(x, gm, shift):
    ms = jnp.mean(x * x, axis=-1, keepdims=True)
    return (x * lax.rsqrt(ms + RMS_EPS)) * gm + shift


def _mod_rows(mod_ref, g_ref, g_row, sub):
    r = N_MOD * sub
    shift = mod_ref[0, 0, r:r + 1, :]
    scale = mod_ref[0, 0, r + 1:r + 2, :]
    gate = mod_ref[0, 0, r + 2:r + 3, :]
    return g_ref[g_row:g_row + 1, :] * (1.0 + scale), shift, gate


def _adaln_kernel(c_ref, w_ref, b_ref, o_ref):
    c = c_ref[...]
    cond = c * jax.nn.sigmoid(c)
    c_hi = cond.astype(BF16)
    c_lo = (cond - c_hi.astype(F32)).astype(BF16)
    w = w_ref[0]
    w_hi = w.astype(BF16)
    w_lo = (w - w_hi.astype(F32)).astype(BF16)
    n = c.shape[0]
    hh = jnp.dot(jnp.concatenate([c_hi, c_lo], axis=0), w_hi, preferred_element_type=F32)
    hl = jnp.dot(c_hi, w_lo, preferred_element_type=F32)
    o_ref[0] = (hh[:n] + hh[n:]) + hl + b_ref[0]


def _adaln(c, ada_w, ada_b):
    n_out = N_SUB * N_MOD * D_MODEL
    tn = 1536
    b, d = c.shape
    return pl.pallas_call(
        _adaln_kernel,
        grid=(DEPTH, n_out // tn),
        in_specs=[
            pl.BlockSpec((b, d), lambda i, n: (0, 0)),
            pl.BlockSpec((1, d, tn), lambda i, n: (i, 0, n)),
            pl.BlockSpec((1, 1, tn), lambda i, n: (i, 0, n)),
        ],
        out_specs=pl.BlockSpec((1, b, tn), lambda i, n: (i, 0, n)),
        out_shape=jax.ShapeDtypeStruct((DEPTH, b, n_out), F32),
        compiler_params=pltpu.CompilerParams(
            dimension_semantics=("parallel", "parallel"), vmem_limit_bytes=VMEM_LIMIT),
        name="adaln",
    )(c, ada_w, ada_b.reshape(DEPTH, 1, n_out))


def _pool_mix(h, hh, pw_ref, head_seen, first):
    ys = []
    for grp, w in enumerate(POOL_WINDOWS):
        lo = grp * POOL_GROUP_DIM
        hg = h[:, lo:lo + POOL_GROUP_DIM]
        acc = jnp.concatenate([hh[:, lo:lo + POOL_GROUP_DIM], hg], axis=0)
        k = 1
        while k < w:
            acc = acc + pltpu.roll(acc, k, 0)
            k *= 2
        win_sum = acc[POOL_HALO:]
        if first:
            inv_cnt = 1.0 / jnp.minimum(head_seen, w).astype(F32)
            pooled = jnp.concatenate(
                [win_sum[:POOL_HALO] * inv_cnt, win_sum[POOL_HALO:] * (1.0 / w)], axis=0)
        else:
            pooled = win_sum * (1.0 / w)
        mixed = pooled - hg
        ys.append(jnp.dot(mixed.astype(BF16), pw_ref[grp], preferred_element_type=F32))
    return jnp.concatenate(ys, axis=-1)


def _ffn_kernel(*refs, sub, g_row, final, pool, ts, n_cast):
    x_ref, mod_ref, g_ref, win_ref, wout_ref = refs[:5]
    rest = list(refs[5:])
    fg_ref = rest.pop(0) if final else None
    pw_ref, ps_ref = (rest.pop(0), rest.pop(0)) if pool else (None, None)
    cast_src, rest = rest[:n_cast], rest[n_cast:]
    o_ref, rest = rest[0], rest[1:]
    cast_dst, rest = rest[:n_cast], rest[n_cast:]
    act_ref = rest.pop(0)
    ptail_ref = rest.pop(0) if pool else None

    for src, dst in zip(cast_src, cast_dst):
        dst[...] = src[...].astype(dst.dtype)

    j = pl.program_id(1)
    tm = x_ref.shape[1]
    n_sub = tm // ts
    gm, shift, gate = _mod_rows(mod_ref, g_ref, g_row, sub)
    half_gate = 0.5 * gate
    if pool:
        p_gm, p_shift, p_gate = _mod_rows(mod_ref, g_ref, g_row + 1, sub + 1)
        p_scale = ps_ref[...] * p_gate
        head_seen = j * tm + 1 + lax.broadcasted_iota(jnp.int32, (POOL_HALO, 1), 0)

        @pl.when(j == 0)
        def _():
            ptail_ref[...] = jnp.zeros_like(ptail_ref)

    def hidden(s):
        hb = _norm_mod(x_ref[0, pl.ds(s * ts, ts), :], gm, shift).astype(BF16)
        for c in range(D_FF // FF_CHUNK):
            lo = c * FF_CHUNK
            a = jnp.dot(hb, win_ref[:, lo:lo + FF_CHUNK], preferred_element_type=F32)
            b = jnp.dot(hb, win_ref[:, D_FF + lo:D_FF + lo + FF_CHUNK],
                        preferred_element_type=F32)
            act_ref[s % 2, :, lo:lo + FF_CHUNK] = (a * jax.nn.sigmoid(a) * b).astype(BF16)

    def project(s):
        y = jnp.dot(act_ref[s % 2], wout_ref[...], preferred_element_type=F32)
        out = x_ref[0, pl.ds(s * ts, ts), :] + half_gate * y
        if final:
            ms = jnp.mean(out * out, axis=-1, keepdims=True)
            out = (out * lax.rsqrt(ms + RMS_EPS)) * fg_ref[...]
        return out

    def mix(s, x1):
        h = _norm_mod(x1, p_gm, p_shift)
        y = _pool_mix(h, ptail_ref[...], pw_ref, head_seen, first=(s == 0))
        ptail_ref[...] = h[ts - POOL_HALO:]
        return x1 + y * p_scale

    if not pool:
        for s in range(n_sub):
            hidden(s)
            o_ref[0, pl.ds(s * ts, ts), :] = project(s)
    else:
        hidden(0)
        x1 = project(0)
        for s in range(1, n_sub):
            hidden(s)
            o_ref[0, pl.ds((s - 1) * ts, ts), :] = mix(s - 1, x1)
            x1 = project(s)
        o_ref[0, pl.ds((n_sub - 1) * ts, ts), :] = mix(n_sub - 1, x1)


def _cast_job(src, lead, n_steps, nj):
    r, c = src.shape[-2:]
    rows = next(k for k in range(BF16_SUBLANES, r + 1, BF16_SUBLANES)
                if r % k == 0 and r // k <= n_steps)
    last = r // rows - 1

    def block(b, j):
        return jnp.minimum(b * nj + j, last)

    src_spec = pl.BlockSpec((None,) * len(lead) + (rows, c),
                            lambda b, j: (*lead, block(b, j), 0))
    dst_spec = pl.BlockSpec((rows, c), lambda b, j: (block(b, j), 0))
    return src_spec, dst_spec, jax.ShapeDtypeStruct((r, c), BF16)


def _ffn(x, mod, norm_g, w_in, w_out, *, layer, sub, tm, ts, final_g=None, pool=None,
         cast=()):
    bsz, t, d = x.shape
    nj = t // tm
    final = final_g is not None
    kern = functools.partial(_ffn_kernel, sub=sub, g_row=N_SUB * layer + sub, final=final,
                             pool=pool is not None, ts=ts, n_cast=len(cast))
    resident = pl.Buffered(1)
    x_spec = pl.BlockSpec((1, tm, d), lambda b, j: (b, j, 0))
    in_specs = [
        x_spec,
        pl.BlockSpec((1, 1, N_SUB * N_MOD, d), lambda b, j: (layer, b, 0, 0)),
        pl.BlockSpec(norm_g.shape, lambda b, j: (0, 0)),
        pl.BlockSpec(w_in.shape, lambda b, j: (0, 0), pipeline_mode=resident),
        pl.BlockSpec(w_out.shape, lambda b, j: (0, 0), pipeline_mode=resident),
    ]
    args = [x, mod, norm_g, w_in, w_out]
    if final:
        in_specs.append(pl.BlockSpec((1, d), lambda b, j: (0, 0)))
        args.append(final_g.reshape(1, d))
    scratch = [pltpu.VMEM((2, ts, D_FF), BF16)]
    if pool is not None:
        pool_w, pool_scale, j_pool = pool
        in_specs += [pl.BlockSpec(pool_w.shape, lambda b, j: (0, 0, 0)),
                     pl.BlockSpec((1, d), lambda b, j: (j_pool, 0))]
        args += [pool_w, pool_scale]
        scratch.append(pltpu.VMEM((POOL_HALO, d), F32))
    out_specs = [x_spec]
    out_shape = [jax.ShapeDtypeStruct(x.shape, x.dtype)]
    for src, lead in cast:
        src_spec, dst_spec, dst_shape = _cast_job(src, lead, bsz * nj, nj)
        in_specs.append(src_spec)
        args.append(src)
        out_specs.append(dst_spec)
        out_shape.append(dst_shape)
    outs = pl.pallas_call(
        kern,
        grid=(bsz, nj),
        in_specs=in_specs,
        out_specs=out_specs,
        out_shape=out_shape,
        scratch_shapes=scratch,
        compiler_params=pltpu.CompilerParams(
            dimension_semantics=("arbitrary", "arbitrary"), vmem_limit_bytes=VMEM_LIMIT),
        name=f"ffn_l{layer}_s{sub}",
    )(*args)
    return outs[0], outs[1:]


def _conv_kernel(x_ref, mod_ref, g_ref, win_ref, cw_ref, wout_ref, o_ref,
                 tail_ref, hb_ref, conv_ref, z_ref, *, sub, g_row, ts):
    j = pl.program_id(1)
    d = D_MODEL
    n_sub = x_ref.shape[1] // ts
    n_col = d // CONV_CHUNK
    gm, shift, gate = _mod_rows(mod_ref, g_ref, g_row, sub)

    @pl.when(j == 0)
    def _():
        tail_ref[...] = jnp.zeros_like(tail_ref)

    def stage_a(s):
        hb = _norm_mod(x_ref[0, pl.ds(s * ts, ts), :], gm, shift).astype(BF16)
        hb_ref[s % 2] = hb
        for n in range(n_col):
            lo = n * CONV_CHUNK
            cg = jnp.dot(hb, win_ref[:, d + lo:d + lo + CONV_CHUNK],
                         preferred_element_type=F32)
            v = jnp.dot(hb, win_ref[:, 2 * d + lo:2 * d + lo + CONV_CHUNK],
                        preferred_element_type=F32)
            u = cg * v
            ext = jnp.concatenate([tail_ref[:, lo:lo + CONV_CHUNK], u], axis=0)
            conv_ref[s % 2, :, lo:lo + CONV_CHUNK] = (
                cw_ref[0, 2:3, lo:lo + CONV_CHUNK] * u
                + cw_ref[0, 1:2, lo:lo + CONV_CHUNK] * pltpu.roll(ext, 1, 0)[CONV_HALO:]
                + cw_ref[0, 0:1, lo:lo + CONV_CHUNK] * pltpu.roll(ext, 2, 0)[CONV_HALO:])
            tail_ref[:, lo:lo + CONV_CHUNK] = u[ts - CONV_HALO:]

    def stage_b(s):
        hb = hb_ref[s % 2]
        for n in range(n_col):
            lo = n * CONV_CHUNK
            bg = jnp.dot(hb, win_ref[:, lo:lo + CONV_CHUNK], preferred_element_type=F32)
            z_ref[s % 2, :, lo:lo + CONV_CHUNK] = (
                bg * conv_ref[s % 2, :, lo:lo + CONV_CHUNK]).astype(BF16)

    def stage_c(s):
        rows = pl.ds(s * ts, ts)
        y = jnp.dot(z_ref[s % 2], wout_ref[...], preferred_element_type=F32)
        o_ref[0, rows, :] = x_ref[0, rows, :] + gate * y

    stage_a(0)
    stage_b(0)
    for s in range(1, n_sub):
        stage_a(s)
        stage_c(s - 1)
        stage_b(s)
    stage_c(n_sub - 1)


def _conv(x, mod, norm_g, w_in, w_conv, w_out, *, layer, j_conv, sub, tm, ts):
    bsz, t, d = x.shape
    kern = functools.partial(_conv_kernel, sub=sub, g_row=N_SUB * layer + sub, ts=ts)
    resident = pl.Buffered(1)
    return pl.pallas_call(
        kern,
        grid=(bsz, t // tm),
        in_specs=[
            pl.BlockSpec((1, tm, d), lambda b, j: (b, j, 0)),
            pl.BlockSpec((1, 1, N_SUB * N_MOD, d), lambda b, j: (layer, b, 0, 0)),
            pl.BlockSpec(norm_g.shape, lambda b, j: (0, 0)),
            pl.BlockSpec(w_in.shape, lambda b, j: (0, 0), pipeline_mode=resident),
            pl.BlockSpec((1, CONV_WIDTH, d), lambda b, j: (j_conv, 0, 0)),
            pl.BlockSpec(w_out.shape, lambda b, j: (0, 0), pipeline_mode=resident),
        ],
        out_specs=pl.BlockSpec((1, tm, d), lambda b, j: (b, j, 0)),
        out_shape=jax.ShapeDtypeStruct(x.shape, x.dtype),
        scratch_shapes=[pltpu.VMEM((CONV_HALO, d), F32), pltpu.VMEM((2, ts, d), BF16),
                        pltpu.VMEM((2, ts, d), F32), pltpu.VMEM((2, ts, d), BF16)],
        compiler_params=pltpu.CompilerParams(
            dimension_semantics=("arbitrary", "arbitrary"), vmem_limit_bytes=VMEM_LIMIT),
        name=f"conv_l{layer}",
    )(x, mod, norm_g, w_in, w_conv, w_out)


def kernel(x, c, ada_w, ada_b, norm_g, ffn_w_in, ffn_w_out, pool_w, pool_scale,
           conv_w_in, conv_w, conv_w_out, final_g):
    bsz, t, d = x.shape
    tm, ts = 2048, 256
    mod = _adaln(c, ada_w, ada_b).reshape(DEPTH, bsz, N_SUB * N_MOD, d)
    norm_g2 = norm_g.reshape(DEPTH * N_SUB, d)
    w_in, w_out = ffn_w_in[0, 0].astype(BF16), ffn_w_out[0, 0].astype(BF16)
    pool_w_b = pool_w.astype(BF16)
    for i in range(DEPTH):
        jm = i // N_MIXERS
        cast = [(ffn_w_in, (i, 1)), (ffn_w_out, (i, 1))]
        if i % N_MIXERS == 0:
            x, (w_in, w_out) = _ffn(x, mod, norm_g2, w_in, w_out, layer=i, sub=0, tm=tm, ts=ts,
                                    pool=(pool_w_b[jm], pool_scale, jm), cast=cast)
        else:
            cast += [(conv_w_in, (jm,)), (conv_w_out, (jm,))]
            x, (w_in, w_out, cw_in, cw_out) = _ffn(x, mod, norm_g2, w_in, w_out, layer=i,
                                                   sub=0, tm=tm, ts=ts, cast=cast)
            x = _conv(x, mod, norm_g2, cw_in, conv_w, cw_out,
                      layer=i, j_conv=jm, sub=1, tm=tm, ts=ts)
        last = i == DEPTH - 1
        cast = [] if last else [(ffn_w_in, (i + 1, 0)), (ffn_w_out, (i + 1, 0))]
        x, nxt = _ffn(x, mod, norm_g2, w_in, w_out, layer=i, sub=2, tm=tm, ts=ts,
                      final_g=final_g if last else None, cast=cast)
        if not last:
            w_in, w_out = nxt
    return x
```

```python
import functools

import jax
import jax.numpy as jnp
from jax import lax
from jax.experimental import pallas as pl
from jax.experimental.pallas import tpu as pltpu

D_MODEL = 1024
DEPTH = 2
N_MIXERS = 2
POOL_WINDOWS = (2, 4, 8, 16)
N_POOL_GROUPS = len(POOL_WINDOWS)
POOL_GROUP_DIM = D_MODEL // N_POOL_GROUPS
CONV_WIDTH = 3
D_FF = ((8 * D_MODEL // 3 + 255) // 256) * 256
N_SUB = 3
N_MOD = 3
RMS_EPS = 1e-6

V7X_MXU_COLS = 256
V7X_VMEM_BYTES = 64 * 1024 * 1024
F32_SUBLANES = 8
BF16_SUBLANES = 16

FF_CHUNK = V7X_MXU_COLS
CONV_CHUNK = V7X_MXU_COLS
POOL_HALO = max(POOL_WINDOWS)
CONV_HALO = F32_SUBLANES
VMEM_SLACK = 4 * 1024 * 1024

F32 = jnp.float32
BF16 = jnp.bfloat16


def _nbytes(shape, dtype):
    n = jnp.dtype(dtype).itemsize
    for k in shape:
        n *= k
    return n


def _vmem_limit(*window_bytes):
    limit = sum(window_bytes) + VMEM_SLACK
    assert limit <= V7X_VMEM_BYTES, limit
    return limit


def _norm_mod(x, gm, shift):
    ms = jnp.mean(x * x, axis=-1, keepdims=True)
    return (x * lax.rsqrt(ms + RMS_EPS)) * gm + shift


def _mod_rows(mod_ref, g_ref, g_row, sub):
    r = N_MOD * sub
    shift = mod_ref[0, 0, r:r + 1, :]
    scale = mod_ref[0, 0, r + 1:r + 2, :]
    gate = mod_ref[0, 0, r + 2:r + 3, :]
    return g_ref[g_row:g_row + 1, :] * (1.0 + scale), shift, gate


def _adaln_kernel(c_ref, w_ref, b_ref, o_ref):
    c = c_ref[...]
    cond = c * jax.nn.sigmoid(c)
    c_hi = cond.astype(BF16)
    c_lo = (cond - c_hi.astype(F32)).astype(BF16)
    w = w_ref[0]
    w_hi = w.astype(BF16)
    w_lo = (w - w_hi.astype(F32)).astype(BF16)
    n = c.shape[0]
    hh = jnp.dot(jnp.concatenate([c_hi, c_lo], axis=0), w_hi, preferred_element_type=F32)
    hl = jnp.dot(c_hi, w_lo, preferred_element_type=F32)
    o_ref[0] = (hh[:n] + hh[n:]) + hl + b_ref[0]


def _adaln(c, ada_w, ada_b):
    n_out = N_SUB * N_MOD * D_MODEL
    tn = 1536
    b, d = c.shape
    return pl.pallas_call(
        _adaln_kernel,
        grid=(DEPTH, n_out // tn),
        in_specs=[
            pl.BlockSpec((b, d), lambda i, n: (0, 0)),
            pl.BlockSpec((1, d, tn), lambda i, n: (i, 0, n)),
            pl.BlockSpec((1, 1, tn), lambda i, n: (i, 0, n)),
        ],
        out_specs=pl.BlockSpec((1, b, tn), lambda i, n: (i, 0, n)),
        out_shape=jax.ShapeDtypeStruct((DEPTH, b, n_out), F32),
        compiler_params=pltpu.CompilerParams(
            dimension_semantics=("parallel", "parallel"),
            vmem_limit_bytes=_vmem_limit(2 * _nbytes((d + 1 + b, tn), F32),
                                         2 * _nbytes((d, tn), F32))),
        name="adaln",
    )(c, ada_w, ada_b.reshape(DEPTH, 1, n_out))


def _pool_mix(h, hh, pw_ref, head_seen, first):
    ys = []
    for grp, w in enumerate(POOL_WINDOWS):
        lo = grp * POOL_GROUP_DIM
        hg = h[:, lo:lo + POOL_GROUP_DIM]
        acc = jnp.concatenate([hh[:, lo:lo + POOL_GROUP_DIM], hg], axis=0)
        k = 1
        while k < w:
            acc = acc + pltpu.roll(acc, k, 0)
            k *= 2
        win_sum = acc[POOL_HALO:]
        if first:
            inv_cnt = 1.0 / jnp.minimum(head_seen, w).astype(F32)
            pooled = jnp.concatenate(
                [win_sum[:POOL_HALO] * inv_cnt, win_sum[POOL_HALO:] * (1.0 / w)], axis=0)
        else:
            pooled = win_sum * (1.0 / w)
        mixed = pooled - hg
        ys.append(jnp.dot(mixed.astype(BF16), pw_ref[grp], preferred_element_type=F32))
    return jnp.concatenate(ys, axis=-1)


def _ffn_kernel(*refs, sub, g_row, final, pool, ts, n_cast):
    x_ref, mod_ref, g_ref, win_ref, wout_ref = refs[:5]
    rest = list(refs[5:])
    fg_ref = rest.pop(0) if final else None
    pw_ref, ps_ref = (rest.pop(0), rest.pop(0)) if pool else (None, None)
    cast_src, rest = rest[:n_cast], rest[n_cast:]
    o_ref, rest = rest[0], rest[1:]
    cast_dst, rest = rest[:n_cast], rest[n_cast:]
    act_ref = rest.pop(0)
    ptail_ref = rest.pop(0) if pool else None

    for src, dst in zip(cast_src, cast_dst):
        dst[...] = src[...].astype(dst.dtype)

    j = pl.program_id(1)
    tm = x_ref.shape[1]
    n_sub = tm // ts
    gm, shift, gate = _mod_rows(mod_ref, g_ref, g_row, sub)
    half_gate = 0.5 * gate
    if pool:
        p_gm, p_shift, p_gate = _mod_rows(mod_ref, g_ref, g_row + 1, sub + 1)
        p_scale = ps_ref[...] * p_gate
        head_seen = j * tm + 1 + lax.broadcasted_iota(jnp.int32, (POOL_HALO, 1), 0)

        @pl.when(j == 0)
        def _():
            ptail_ref[...] = jnp.zeros_like(ptail_ref)

    def hidden(s):
        hb = _norm_mod(x_ref[0, pl.ds(s * ts, ts), :], gm, shift).astype(BF16)
        for c in range(D_FF // FF_CHUNK):
            lo = c * FF_CHUNK
            a = jnp.dot(hb, win_ref[:, lo:lo + FF_CHUNK], preferred_element_type=F32)
            b = jnp.dot(hb, win_ref[:, D_FF + lo:D_FF + lo + FF_CHUNK],
                        preferred_element_type=F32)
            act_ref[s % 2, :, lo:lo + FF_CHUNK] = (a * jax.nn.sigmoid(a) * b).astype(BF16)

    def project(s):
        y = jnp.dot(act_ref[s % 2], wout_ref[...], preferred_element_type=F32)
        out = x_ref[0, pl.ds(s * ts, ts), :] + half_gate * y
        if final:
            ms = jnp.mean(out * out, axis=-1, keepdims=True)
            out = (out * lax.rsqrt(ms + RMS_EPS)) * fg_ref[...]
        return out

    def mix(s, x1):
        h = _norm_mod(x1, p_gm, p_shift)
        y = _pool_mix(h, ptail_ref[...], pw_ref, head_seen, first=(s == 0))
        ptail_ref[...] = h[ts - POOL_HALO:]
        return x1 + y * p_scale

    if not pool:
        for s in range(n_sub):
            hidden(s)
            o_ref[0, pl.ds(s * ts, ts), :] = project(s)
    else:
        hidden(0)
        x1 = project(0)
        for s in range(1, n_sub):
            hidden(s)
            o_ref[0, pl.ds((s - 1) * ts, ts), :] = mix(s - 1, x1)
            x1 = project(s)
        o_ref[0, pl.ds((n_sub - 1) * ts, ts), :] = mix(n_sub - 1, x1)


def _cast_job(src, lead, n_steps, nj):
    r, c = src.shape[-2:]
    rows = next(k for k in range(BF16_SUBLANES, r + 1, BF16_SUBLANES)
                if r % k == 0 and r // k <= n_steps)
    last = r // rows - 1

    def block(b, j):
        return jnp.minimum(b * nj + j, last)

    src_spec = pl.BlockSpec((None,) * len(lead) + (rows, c),
                            lambda b, j: (*lead, block(b, j), 0))
    dst_spec = pl.BlockSpec((rows, c), lambda b, j: (block(b, j), 0))
    return src_spec, dst_spec, jax.ShapeDtypeStruct((r, c), BF16)


def _ffn(x, mod, norm_g, w_in, w_out, *, layer, sub, tm, ts, final_g=None, pool=None,
         cast=()):
    bsz, t, d = x.shape
    nj = t // tm
    final = final_g is not None
    kern = functools.partial(_ffn_kernel, sub=sub, g_row=N_SUB * layer + sub, final=final,
                             pool=pool is not None, ts=ts, n_cast=len(cast))
    resident = pl.Buffered(1)
    x_spec = pl.BlockSpec((1, tm, d), lambda b, j: (b, j, 0))
    in_specs = [
        x_spec,
        pl.BlockSpec((1, 1, N_SUB * N_MOD, d), lambda b, j: (layer, b, 0, 0)),
        pl.BlockSpec(norm_g.shape, lambda b, j: (0, 0)),
        pl.BlockSpec(w_in.shape, lambda b, j: (0, 0), pipeline_mode=resident),
        pl.BlockSpec(w_out.shape, lambda b, j: (0, 0), pipeline_mode=resident),
    ]
    args = [x, mod, norm_g, w_in, w_out]
    if final:
        in_specs.append(pl.BlockSpec((1, d), lambda b, j: (0, 0)))
        args.append(final_g.reshape(1, d))
    scratch = [pltpu.VMEM((2, ts, D_FF), BF16)]
    vmem = [4 * _nbytes((tm, d), F32), _nbytes(w_in.shape, BF16), _nbytes(w_out.shape, BF16),
            _nbytes((2, ts, D_FF), BF16)]
    if pool is not None:
        pool_w, pool_scale, j_pool = pool
        in_specs += [pl.BlockSpec(pool_w.shape, lambda b, j: (0, 0, 0)),
                     pl.BlockSpec((1, d), lambda b, j: (j_pool, 0))]
        args += [pool_w, pool_scale]
        scratch.append(pltpu.VMEM((POOL_HALO, d), F32))
        vmem.append(2 * _nbytes(pool_w.shape, BF16))
    out_specs = [x_spec]
    out_shape = [jax.ShapeDtypeStruct(x.shape, x.dtype)]
    for src, lead in cast:
        src_spec, dst_spec, dst_shape = _cast_job(src, lead, bsz * nj, nj)
        in_specs.append(src_spec)
        args.append(src)
        out_specs.append(dst_spec)
        out_shape.append(dst_shape)
        vmem += [2 * _nbytes(dst_spec.block_shape, F32), 2 * _nbytes(dst_spec.block_shape, BF16)]
    outs = pl.pallas_call(
        kern,
        grid=(bsz, nj),
        in_specs=in_specs,
        out_specs=out_specs,
        out_shape=out_shape,
        scratch_shapes=scratch,
        compiler_params=pltpu.CompilerParams(
            dimension_semantics=("arbitrary", "arbitrary"), vmem_limit_bytes=_vmem_limit(*vmem)),
        name=f"ffn_l{layer}_s{sub}",
    )(*args)
    return outs[0], outs[1:]


def _conv_kernel(x_ref, mod_ref, g_ref, win_ref, cw_ref, wout_ref, o_ref,
                 tail_ref, hb_ref, conv_ref, z_ref, *, sub, g_row, ts):
    j = pl.program_id(1)
    d = D_MODEL
    n_sub = x_ref.shape[1] // ts
    n_col = d // CONV_CHUNK
    gm, shift, gate = _mod_rows(mod_ref, g_ref, g_row, sub)

    @pl.when(j == 0)
    def _():
        tail_ref[...] = jnp.zeros_like(tail_ref)

    def stage_a(s):
        hb = _norm_mod(x_ref[0, pl.ds(s * ts, ts), :], gm, shift).astype(BF16)
        hb_ref[s % 2] = hb
        for n in range(n_col):
            lo = n * CONV_CHUNK
            cg = jnp.dot(hb, win_ref[:, d + lo:d + lo + CONV_CHUNK],
                         preferred_element_type=F32)
            v = jnp.dot(hb, win_ref[:, 2 * d + lo:2 * d + lo + CONV_CHUNK],
                        preferred_element_type=F32)
            u = cg * v
            ext = jnp.concatenate([tail_ref[:, lo:lo + CONV_CHUNK], u], axis=0)
            conv_ref[s % 2, :, lo:lo + CONV_CHUNK] = (
                cw_ref[0, 2:3, lo:lo + CONV_CHUNK] * u
                + cw_ref[0, 1:2, lo:lo + CONV_CHUNK] * pltpu.roll(ext, 1, 0)[CONV_HALO:]
                + cw_ref[0, 0:1, lo:lo + CONV_CHUNK] * pltpu.roll(ext, 2, 0)[CONV_HALO:])
            tail_ref[:, lo:lo + CONV_CHUNK] = u[ts - CONV_HALO:]

    def stage_b(s):
        hb = hb_ref[s % 2]
        for n in range(n_col):
            lo = n * CONV_CHUNK
            bg = jnp.dot(hb, win_ref[:, lo:lo + CONV_CHUNK], preferred_element_type=F32)
            z_ref[s % 2, :, lo:lo + CONV_CHUNK] = (
                bg * conv_ref[s % 2, :, lo:lo + CONV_CHUNK]).astype(BF16)

    def stage_c(s):
        rows = pl.ds(s * ts, ts)
        y = jnp.dot(z_ref[s % 2], wout_ref[...], preferred_element_type=F32)
        o_ref[0, rows, :] = x_ref[0, rows, :] + gate * y

    stage_a(0)
    stage_b(0)
    for s in range(1, n_sub):
        stage_a(s)
        stage_c(s - 1)
        stage_b(s)
    stage_c(n_sub - 1)


def _conv(x, mod, norm_g, w_in, w_conv, w_out, *, layer, j_conv, sub, tm, ts):
    bsz, t, d = x.shape
    kern = functools.partial(_conv_kernel, sub=sub, g_row=N_SUB * layer + sub, ts=ts)
    resident = pl.Buffered(1)
    return pl.pallas_call(
        kern,
        grid=(bsz, t // tm),
        in_specs=[
            pl.BlockSpec((1, tm, d), lambda b, j: (b, j, 0)),
            pl.BlockSpec((1, 1, N_SUB * N_MOD, d), lambda b, j: (layer, b, 0, 0)),
            pl.BlockSpec(norm_g.shape, lambda b, j: (0, 0)),
            pl.BlockSpec(w_in.shape, lambda b, j: (0, 0), pipeline_mode=resident),
            pl.BlockSpec((1, CONV_WIDTH, d), lambda b, j: (j_conv, 0, 0)),
            pl.BlockSpec(w_out.shape, lambda b, j: (0, 0), pipeline_mode=resident),
        ],
        out_specs=pl.BlockSpec((1, tm, d), lambda b, j: (b, j, 0)),
        out_shape=jax.ShapeDtypeStruct(x.shape, x.dtype),
        scratch_shapes=[pltpu.VMEM((CONV_HALO, d), F32), pltpu.VMEM((2, ts, d), BF16),
                        pltpu.VMEM((2, ts, d), F32), pltpu.VMEM((2, ts, d), BF16)],
        compiler_params=pltpu.CompilerParams(
            dimension_semantics=("arbitrary", "arbitrary"),
            vmem_limit_bytes=_vmem_limit(
                4 * _nbytes((tm, d), F32), _nbytes(w_in.shape, BF16), _nbytes(w_out.shape, BF16),
                _nbytes((CONV_HALO, d), F32), 2 * _nbytes((2, ts, d), BF16),
                _nbytes((2, ts, d), F32))),
        name=f"conv_l{layer}",
    )(x, mod, norm_g, w_in, w_conv, w_out)


def kernel(x, c, ada_w, ada_b, norm_g, ffn_w_in, ffn_w_out, pool_w, pool_scale,
           conv_w_in, conv_w, conv_w_out, final_g):
    bsz, t, d = x.shape
    tm, ts = 1024, 256
    mod = _adaln(c, ada_w, ada_b).reshape(DEPTH, bsz, N_SUB * N_MOD, d)
    norm_g2 = norm_g.reshape(DEPTH * N_SUB, d)
    w_in, w_out = ffn_w_in[0, 0].astype(BF16), ffn_w_out[0, 0].astype(BF16)
    pool_w_b = pool_w.astype(BF16)
    for i in range(DEPTH):
        jm = i // N_MIXERS
        cast = [(ffn_w_in, (i, 1)), (ffn_w_out, (i, 1))]
        if i % N_MIXERS == 0:
            x, (w_in, w_out) = _ffn(x, mod, norm_g2, w_in, w_out, layer=i, sub=0, tm=tm, ts=ts,
                                    pool=(pool_w_b[jm], pool_scale, jm), cast=cast)
        else:
            cast += [(conv_w_in, (jm,)), (conv_w_out, (jm,))]
            x, (w_in, w_out, cw_in, cw_out) = _ffn(x, mod, norm_g2, w_in, w_out, layer=i,
                                                   sub=0, tm=tm, ts=ts, cast=cast)
            x = _conv(x, mod, norm_g2, cw_in, conv_w, cw_out,
                      layer=i, j_conv=jm, sub=1, tm=tm, ts=ts)
        last = i == DEPTH - 1
        cast = [] if last else [(ffn_w_in, (i + 1, 0)), (ffn_w_out, (i + 1, 0))]
        x, nxt = _ffn(x, mod, norm_g2, w_in, w_out, layer=i, sub=2, tm=tm, ts=ts,
                      final_g=final_g if last else None, cast=cast)
        if not last:
            w_in, w_out = nxt
    return x
```

```python
import functools

import jax
import jax.numpy as jnp
from jax import lax
from jax.experimental import pallas as pl
from jax.experimental.pallas import tpu as pltpu

D_MODEL = 1024
DEPTH = 2
N_MIXERS = 2
POOL_WINDOWS = (2, 4, 8, 16)
N_POOL_GROUPS = len(POOL_WINDOWS)
POOL_GROUP_DIM = D_MODEL // N_POOL_GROUPS
CONV_WIDTH = 3
D_FF = ((8 * D_MODEL // 3 + 255) // 256) * 256
N_SUB = 3
N_MOD = 3
RMS_EPS = 1e-6

V7X_MXU_COLS = 256
V7X_VMEM_BYTES = 64 * 1024 * 1024
F32_SUBLANES = 8
BF16_SUBLANES = 16

FF_CHUNK = V7X_MXU_COLS
CONV_CHUNK = V7X_MXU_COLS
POOL_HALO = max(POOL_WINDOWS)
CONV_HALO = F32_SUBLANES
VMEM_SLACK = 4 * 1024 * 1024

F32 = jnp.float32
BF16 = jnp.bfloat16


def _nbytes(shape, dtype):
    n = jnp.dtype(dtype).itemsize
    for k in shape:
        n *= k
    return n


def _vmem_limit(*window_bytes):
    limit = sum(window_bytes) + VMEM_SLACK
    assert limit <= V7X_VMEM_BYTES, limit
    return limit


def _norm_mod(x, gm, shift):
    ms = jnp.mean(x * x, axis=-1, keepdims=True)
    return (x * lax.rsqrt(ms + RMS_EPS)) * gm + shift


def _mod_rows(mod_ref, g_ref, g_row, sub):
    r = N_MOD * sub
    shift = mod_ref[0, 0, r:r + 1, :]
    scale = mod_ref[0, 0, r + 1:r + 2, :]
    gate = mod_ref[0, 0, r + 2:r + 3, :]
    return g_ref[g_row:g_row + 1, :] * (1.0 + scale), shift, gate


def _adaln_kernel(c_ref, w_ref, b_ref, o_ref):
    c = c_ref[...]
    cond = c * jax.nn.sigmoid(c)
    c_hi = cond.astype(BF16)
    c_lo = (cond - c_hi.astype(F32)).astype(BF16)
    w = w_ref[0]
    w_hi = w.astype(BF16)
    w_lo = (w - w_hi.astype(F32)).astype(BF16)
    n = c.shape[0]
    hh = jnp.dot(jnp.concatenate([c_hi, c_lo], axis=0), w_hi, preferred_element_type=F32)
    hl = jnp.dot(c_hi, w_lo, preferred_element_type=F32)
    o_ref[0] = (hh[:n] + hh[n:]) + hl + b_ref[0]


def _adaln(c, ada_w, ada_b):
    n_out = N_SUB * N_MOD * D_MODEL
    tn = 1536
    b, d = c.shape
    return pl.pallas_call(
        _adaln_kernel,
        grid=(DEPTH, n_out // tn),
        in_specs=[
            pl.BlockSpec((b, d), lambda i, n: (0, 0)),
            pl.BlockSpec((1, d, tn), lambda i, n: (i, 0, n)),
            pl.BlockSpec((1, 1, tn), lambda i, n: (i, 0, n)),
        ],
        out_specs=pl.BlockSpec((1, b, tn), lambda i, n: (i, 0, n)),
        out_shape=jax.ShapeDtypeStruct((DEPTH, b, n_out), F32),
        compiler_params=pltpu.CompilerParams(
            dimension_semantics=("parallel", "parallel"),
            vmem_limit_bytes=_vmem_limit(2 * _nbytes((d + 1 + b, tn), F32),
                                         2 * _nbytes((d, tn), F32))),
        name="adaln",
    )(c, ada_w, ada_b.reshape(DEPTH, 1, n_out))


def _pool_mix(h, hh, pw_ref, head_seen, first):
    ys = []
    for grp, w in enumerate(POOL_WINDOWS):
        lo = grp * POOL_GROUP_DIM
        hg = h[:, lo:lo + POOL_GROUP_DIM]
        acc = jnp.concatenate([hh[:, lo:lo + POOL_GROUP_DIM], hg], axis=0)
        k = 1
        while k < w:
            acc = acc + pltpu.roll(acc, k, 0)
            k *= 2
        win_sum = acc[POOL_HALO:]
        if first:
            inv_cnt = 1.0 / jnp.minimum(head_seen, w).astype(F32)
            pooled = jnp.concatenate(
                [win_sum[:POOL_HALO] * inv_cnt, win_sum[POOL_HALO:] * (1.0 / w)], axis=0)
        else:
            pooled = win_sum * (1.0 / w)
        mixed = pooled - hg
        ys.append(jnp.dot(mixed.astype(BF16), pw_ref[grp], preferred_element_type=F32))
    return jnp.concatenate(ys, axis=-1)


def _ffn_kernel(*refs, sub, g_row, final, pool, conv, ts, n_cast):
    x_ref, mod_ref, g_ref, win_ref, wout_ref = refs[:5]
    rest = list(refs[5:])
    fg_ref = rest.pop(0) if final else None
    pw_ref, ps_ref = (rest.pop(0), rest.pop(0)) if pool else (None, None)
    cwin_ref, cw_ref, cwout_ref = (rest.pop(0), rest.pop(0), rest.pop(0)) if conv else (None,) * 3
    cast_src, rest = rest[:n_cast], rest[n_cast:]
    o_ref, rest = rest[0], rest[1:]
    cast_dst, rest = rest[:n_cast], rest[n_cast:]
    act_ref = rest.pop(0)
    ptail_ref = rest.pop(0) if pool else None
    if conv:
        ctail_ref, hbc_ref, cconv_ref, z_ref, x1_ref = rest

    for src, dst in zip(cast_src, cast_dst):
        dst[...] = src[...].astype(dst.dtype)

    j = pl.program_id(1)
    tm = x_ref.shape[1]
    n_sub = tm // ts
    gm, shift, gate = _mod_rows(mod_ref, g_ref, g_row, sub)
    half_gate = 0.5 * gate
    if pool:
        p_gm, p_shift, p_gate = _mod_rows(mod_ref, g_ref, g_row + 1, sub + 1)
        p_scale = ps_ref[...] * p_gate
        head_seen = j * tm + 1 + lax.broadcasted_iota(jnp.int32, (POOL_HALO, 1), 0)

        @pl.when(j == 0)
        def _():
            ptail_ref[...] = jnp.zeros_like(ptail_ref)

    def hidden(s):
        hb = _norm_mod(x_ref[0, pl.ds(s * ts, ts), :], gm, shift).astype(BF16)
        for c in range(D_FF // FF_CHUNK):
            lo = c * FF_CHUNK
            a = jnp.dot(hb, win_ref[:, lo:lo + FF_CHUNK], preferred_element_type=F32)
            b = jnp.dot(hb, win_ref[:, D_FF + lo:D_FF + lo + FF_CHUNK],
                        preferred_element_type=F32)
            act_ref[s % 2, :, lo:lo + FF_CHUNK] = (a * jax.nn.sigmoid(a) * b).astype(BF16)

    def project(s):
        y = jnp.dot(act_ref[s % 2], wout_ref[...], preferred_element_type=F32)
        out = x_ref[0, pl.ds(s * ts, ts), :] + half_gate * y
        if final:
            ms = jnp.mean(out * out, axis=-1, keepdims=True)
            out = (out * lax.rsqrt(ms + RMS_EPS)) * fg_ref[...]
        return out

    def mix(s, x1):
        h = _norm_mod(x1, p_gm, p_shift)
        y = _pool_mix(h, ptail_ref[...], pw_ref, head_seen, first=(s == 0))
        ptail_ref[...] = h[ts - POOL_HALO:]
        return x1 + y * p_scale

    if conv:
        d = D_MODEL
        n_col = d // CONV_CHUNK
        c_gm, c_shift, c_gate = _mod_rows(mod_ref, g_ref, g_row + 1, sub + 1)

        @pl.when(j == 0)
        def _():
            ctail_ref[...] = jnp.zeros_like(ctail_ref)

        def conv_a(s):
            hbc = _norm_mod(x1_ref[s % 2], c_gm, c_shift).astype(BF16)
            hbc_ref[...] = hbc
            for n in range(n_col):
                lo = n * CONV_CHUNK
                cg = jnp.dot(hbc, cwin_ref[:, d + lo:d + lo + CONV_CHUNK],
                             preferred_element_type=F32)
                v = jnp.dot(hbc, cwin_ref[:, 2 * d + lo:2 * d + lo + CONV_CHUNK],
                            preferred_element_type=F32)
                u = cg * v
                ext = jnp.concatenate([ctail_ref[:, lo:lo + CONV_CHUNK], u], axis=0)
                cconv_ref[:, lo:lo + CONV_CHUNK] = (
                    cw_ref[0, 2:3, lo:lo + CONV_CHUNK] * u
                    + cw_ref[0, 1:2, lo:lo + CONV_CHUNK] * pltpu.roll(ext, 1, 0)[CONV_HALO:]
                    + cw_ref[0, 0:1, lo:lo + CONV_CHUNK] * pltpu.roll(ext, 2, 0)[CONV_HALO:])
                ctail_ref[:, lo:lo + CONV_CHUNK] = u[ts - CONV_HALO:]

        def conv_b(s):
            hbc = hbc_ref[...]
            for n in range(n_col):
                lo = n * CONV_CHUNK
                bg = jnp.dot(hbc, cwin_ref[:, lo:lo + CONV_CHUNK], preferred_element_type=F32)
                z_ref[:, lo:lo + CONV_CHUNK] = (bg * cconv_ref[:, lo:lo + CONV_CHUNK]).astype(BF16)

        def conv_c(s):
            y2 = jnp.dot(z_ref[...], cwout_ref[...], preferred_element_type=F32)
            o_ref[0, pl.ds(s * ts, ts), :] = x1_ref[s % 2] + c_gate * y2

        hidden(0)
        x1_ref[0] = project(0)
        for s in range(1, n_sub):
            hidden(s)
            conv_a(s - 1)
            conv_b(s - 1)
            x1_ref[s % 2] = project(s)
            conv_c(s - 1)
        conv_a(n_sub - 1)
        conv_b(n_sub - 1)
        conv_c(n_sub - 1)
    elif pool:
        hidden(0)
        x1 = project(0)
        for s in range(1, n_sub):
            hidden(s)
            o_ref[0, pl.ds((s - 1) * ts, ts), :] = mix(s - 1, x1)
            x1 = project(s)
        o_ref[0, pl.ds((n_sub - 1) * ts, ts), :] = mix(n_sub - 1, x1)
    else:
        for s in range(n_sub):
            hidden(s)
            o_ref[0, pl.ds(s * ts, ts), :] = project(s)


def _cast_job(src, lead, n_steps, nj):
    r, c = src.shape[-2:]
    rows = next(k for k in range(BF16_SUBLANES, r + 1, BF16_SUBLANES)
                if r % k == 0 and r // k <= n_steps)
    last = r // rows - 1

    def block(b, j):
        return jnp.minimum(b * nj + j, last)

    src_spec = pl.BlockSpec((None,) * len(lead) + (rows, c),
                            lambda b, j: (*lead, block(b, j), 0))
    dst_spec = pl.BlockSpec((rows, c), lambda b, j: (block(b, j), 0))
    return src_spec, dst_spec, jax.ShapeDtypeStruct((r, c), BF16)


def _ffn(x, mod, norm_g, w_in, w_out, *, layer, sub, tm, ts, final_g=None, pool=None,
         conv=None, cast=()):
    bsz, t, d = x.shape
    nj = t // tm
    final = final_g is not None
    kern = functools.partial(_ffn_kernel, sub=sub, g_row=N_SUB * layer + sub, final=final,
                             pool=pool is not None, conv=conv is not None, ts=ts,
                             n_cast=len(cast))
    resident = pl.Buffered(1)
    x_spec = pl.BlockSpec((1, tm, d), lambda b, j: (b, j, 0))
    in_specs = [
        x_spec,
        pl.BlockSpec((1, 1, N_SUB * N_MOD, d), lambda b, j: (layer, b, 0, 0)),
        pl.BlockSpec(norm_g.shape, lambda b, j: (0, 0)),
        pl.BlockSpec(w_in.shape, lambda b, j: (0, 0), pipeline_mode=resident),
        pl.BlockSpec(w_out.shape, lambda b, j: (0, 0), pipeline_mode=resident),
    ]
    args = [x, mod, norm_g, w_in, w_out]
    if final:
        in_specs.append(pl.BlockSpec((1, d), lambda b, j: (0, 0)))
        args.append(final_g.reshape(1, d))
    scratch = [pltpu.VMEM((2, ts, D_FF), BF16)]
    vmem = [4 * _nbytes((tm, d), F32), _nbytes(w_in.shape, BF16), _nbytes(w_out.shape, BF16),
            _nbytes((2, ts, D_FF), BF16)]
    if pool is not None:
        pool_w, pool_scale, j_pool = pool
        in_specs += [pl.BlockSpec(pool_w.shape, lambda b, j: (0, 0, 0)),
                     pl.BlockSpec((1, d), lambda b, j: (j_pool, 0))]
        args += [pool_w, pool_scale]
        scratch.append(pltpu.VMEM((POOL_HALO, d), F32))
        vmem.append(2 * _nbytes(pool_w.shape, BF16))
    if conv is not None:
        cw_in, cw, cw_out, j_conv = conv
        in_specs += [pl.BlockSpec(cw_in.shape, lambda b, j: (0, 0), pipeline_mode=resident),
                     pl.BlockSpec((1, CONV_WIDTH, d), lambda b, j: (j_conv, 0, 0)),
                     pl.BlockSpec(cw_out.shape, lambda b, j: (0, 0), pipeline_mode=resident)]
        args += [cw_in, cw, cw_out]
        scratch += [pltpu.VMEM((CONV_HALO, d), F32), pltpu.VMEM((ts, d), BF16),
                    pltpu.VMEM((ts, d), F32), pltpu.VMEM((ts, d), BF16),
                    pltpu.VMEM((2, ts, d), F32)]
        vmem += [_nbytes(cw_in.shape, BF16), _nbytes(cw_out.shape, BF16),
                 _nbytes((CONV_HALO, d), F32), 2 * _nbytes((ts, d), BF16),
                 _nbytes((3, ts, d), F32)]
    out_specs = [x_spec]
    out_shape = [jax.ShapeDtypeStruct(x.shape, x.dtype)]
    for src, lead in cast:
        src_spec, dst_spec, dst_shape = _cast_job(src, lead, bsz * nj, nj)
        in_specs.append(src_spec)
        args.append(src)
        out_specs.append(dst_spec)
        out_shape.append(dst_shape)
        vmem += [2 * _nbytes(dst_spec.block_shape, F32), 2 * _nbytes(dst_spec.block_shape, BF16)]
    outs = pl.pallas_call(
        kern,
        grid=(bsz, nj),
        in_specs=in_specs,
        out_specs=out_specs,
        out_shape=out_shape,
        scratch_shapes=scratch,
        compiler_params=pltpu.CompilerParams(
            dimension_semantics=("arbitrary", "arbitrary"), vmem_limit_bytes=_vmem_limit(*vmem)),
        name=f"ffn_l{layer}_s{sub}",
    )(*args)
    return outs[0], outs[1:]


def kernel(x, c, ada_w, ada_b, norm_g, ffn_w_in, ffn_w_out, pool_w, pool_scale,
           conv_w_in, conv_w, conv_w_out, final_g):
    bsz, t, d = x.shape
    tm, ts = 1024, 256
    mod = _adaln(c, ada_w, ada_b).reshape(DEPTH, bsz, N_SUB * N_MOD, d)
    norm_g2 = norm_g.reshape(DEPTH * N_SUB, d)
    w_in, w_out = ffn_w_in[0, 0].astype(BF16), ffn_w_out[0, 0].astype(BF16)
    pool_w_b = pool_w.astype(BF16)
    cw_in = cw_out = None
    for i in range(DEPTH):
        jm = i // N_MIXERS
        cast = [(ffn_w_in, (i, 1)), (ffn_w_out, (i, 1))]
        if i % N_MIXERS == 0:
            mixer = dict(pool=(pool_w_b[jm], pool_scale, jm))
        else:
            mixer = dict(conv=(cw_in, conv_w, cw_out, jm))
        x, (w_in, w_out) = _ffn(x, mod, norm_g2, w_in, w_out, layer=i, sub=0, tm=tm, ts=ts,
                                cast=cast, **mixer)
        last = i == DEPTH - 1
        cast = [] if last else [(ffn_w_in, (i + 1, 0)), (ffn_w_out, (i + 1, 0))]
        if not last and (i + 1) % N_MIXERS != 0:
            jn = (i + 1) // N_MIXERS
            cast += [(conv_w_in, (jn,)), (conv_w_out, (jn,))]
        x, nxt = _ffn(x, mod, norm_g2, w_in, w_out, layer=i, sub=2, tm=tm, ts=ts,
                      final_g=final_g if last else None, cast=cast)
        if not last:
            w_in, w_out, *conv_b = nxt
            if conv_b:
                cw_in, cw_out = conv_b
    return x
```

```python
import functools

import jax
import jax.numpy as jnp
from jax import lax
from jax.experimental import pallas as pl
from jax.experimental.pallas import tpu as pltpu

D_MODEL = 1024
DEPTH = 2
N_MIXERS = 2
POOL_WINDOWS = (2, 4, 8, 16)
N_POOL_GROUPS = len(POOL_WINDOWS)
POOL_GROUP_DIM = D_MODEL // N_POOL_GROUPS
CONV_WIDTH = 3
D_FF = ((8 * D_MODEL // 3 + 255) // 256) * 256
N_SUB = 3
N_MOD = 3
RMS_EPS = 1e-6

V7X_MXU_COLS = 256
V7X_VMEM_BYTES = 64 * 1024 * 1024
F32_SUBLANES = 8
BF16_SUBLANES = 16

FF_CHUNK = V7X_MXU_COLS
CONV_CHUNK = V7X_MXU_COLS
POOL_HALO = max(POOL_WINDOWS)
CONV_HALO = F32_SUBLANES
VMEM_SLACK = 4 * 1024 * 1024

F32 = jnp.float32
BF16 = jnp.bfloat16


def _nbytes(shape, dtype):
    n = jnp.dtype(dtype).itemsize
    for k in shape:
        n *= k
    return n


def _vmem_limit(*window_bytes):
    limit = sum(window_bytes) + VMEM_SLACK
    assert limit <= V7X_VMEM_BYTES, limit
    return limit


def _norm_mod(x, gm, shift):
    ms = jnp.mean(x * x, axis=-1, keepdims=True)
    return (x * lax.rsqrt(ms + RMS_EPS)) * gm + shift


def _mod_rows(mod_ref, g_ref, g_row, sub):
    r = N_MOD * sub
    shift = mod_ref[0, 0, r:r + 1, :]
    scale = mod_ref[0, 0, r + 1:r + 2, :]
    gate = mod_ref[0, 0, r + 2:r + 3, :]
    return g_ref[g_row:g_row + 1, :] * (1.0 + scale), shift, gate


def _adaln_kernel(c_ref, w_ref, b_ref, o_ref):
    c = c_ref[...]
    cond = c * jax.nn.sigmoid(c)
    c_hi = cond.astype(BF16)
    c_lo = (cond - c_hi.astype(F32)).astype(BF16)
    w = w_ref[0]
    w_hi = w.astype(BF16)
    w_lo = (w - w_hi.astype(F32)).astype(BF16)
    n = c.shape[0]
    hh = jnp.dot(jnp.concatenate([c_hi, c_lo], axis=0), w_hi, preferred_element_type=F32)
    hl = jnp.dot(c_hi, w_lo, preferred_element_type=F32)
    o_ref[0] = (hh[:n] + hh[n:]) + hl + b_ref[0]


def _adaln(c, ada_w, ada_b):
    n_out = N_SUB * N_MOD * D_MODEL
    tn = 1536
    b, d = c.shape
    return pl.pallas_call(
        _adaln_kernel,
        grid=(DEPTH, n_out // tn),
        in_specs=[
            pl.BlockSpec((b, d), lambda i, n: (0, 0)),
            pl.BlockSpec((1, d, tn), lambda i, n: (i, 0, n)),
            pl.BlockSpec((1, 1, tn), lambda i, n: (i, 0, n)),
        ],
        out_specs=pl.BlockSpec((1, b, tn), lambda i, n: (i, 0, n)),
        out_shape=jax.ShapeDtypeStruct((DEPTH, b, n_out), F32),
        compiler_params=pltpu.CompilerParams(
            dimension_semantics=("parallel", "parallel"),
            vmem_limit_bytes=_vmem_limit(2 * _nbytes((d + 1 + b, tn), F32),
                                         2 * _nbytes((d, tn), F32))),
        name="adaln",
    )(c, ada_w, ada_b.reshape(DEPTH, 1, n_out))


def _pool_mix(h, hh, pw_ref, head_seen, first):
    ys = []
    for grp, w in enumerate(POOL_WINDOWS):
        lo = grp * POOL_GROUP_DIM
        hg = h[:, lo:lo + POOL_GROUP_DIM]
        acc = jnp.concatenate([hh[:, lo:lo + POOL_GROUP_DIM], hg], axis=0)
        k = 1
        while k < w:
            acc = acc + pltpu.roll(acc, k, 0)
            k *= 2
        win_sum = acc[POOL_HALO:]
        if first:
            inv_cnt = 1.0 / jnp.minimum(head_seen, w).astype(F32)
            pooled = jnp.concatenate(
                [win_sum[:POOL_HALO] * inv_cnt, win_sum[POOL_HALO:] * (1.0 / w)], axis=0)
        else:
            pooled = win_sum * (1.0 / w)
        mixed = pooled - hg
        ys.append(jnp.dot(mixed.astype(BF16), pw_ref[grp], preferred_element_type=F32))
    return jnp.concatenate(ys, axis=-1)


def _ffn_kernel(*refs, sub, g_row, final, pool, ts, n_cast):
    x_ref, mod_ref, g_ref, win_ref, wout_ref = refs[:5]
    rest = list(refs[5:])
    fg_ref = rest.pop(0) if final else None
    pw_ref, ps_ref = (rest.pop(0), rest.pop(0)) if pool else (None, None)
    cast_src, rest = rest[:n_cast], rest[n_cast:]
    o_ref, rest = rest[0], rest[1:]
    cast_dst, rest = rest[:n_cast], rest[n_cast:]
    act_ref = rest.pop(0)
    ptail_ref = rest.pop(0) if pool else None

    for src, dst in zip(cast_src, cast_dst):
        dst[...] = src[...].astype(dst.dtype)

    j = pl.program_id(1)
    tm = x_ref.shape[1]
    n_sub = tm // ts
    gm, shift, gate = _mod_rows(mod_ref, g_ref, g_row, sub)
    half_gate = 0.5 * gate
    if pool:
        p_gm, p_shift, p_gate = _mod_rows(mod_ref, g_ref, g_row + 1, sub + 1)
        p_scale = ps_ref[...] * p_gate
        head_seen = j * tm + 1 + lax.broadcasted_iota(jnp.int32, (POOL_HALO, 1), 0)

        @pl.when(j == 0)
        def _():
            ptail_ref[...] = jnp.zeros_like(ptail_ref)

    def hidden(s):
        hb = _norm_mod(x_ref[0, pl.ds(s * ts, ts), :], gm, shift).astype(BF16)
        for c in range(D_FF // FF_CHUNK):
            lo = c * FF_CHUNK
            a = jnp.dot(hb, win_ref[:, lo:lo + FF_CHUNK], preferred_element_type=F32)
            b = jnp.dot(hb, win_ref[:, D_FF + lo:D_FF + lo + FF_CHUNK],
                        preferred_element_type=F32)
            act_ref[s % 2, :, lo:lo + FF_CHUNK] = (a * jax.nn.sigmoid(a) * b).astype(BF16)

    def project(s):
        y = jnp.dot(act_ref[s % 2], wout_ref[...], preferred_element_type=F32)
        out = x_ref[0, pl.ds(s * ts, ts), :] + half_gate * y
        if final:
            ms = jnp.mean(out * out, axis=-1, keepdims=True)
            out = (out * lax.rsqrt(ms + RMS_EPS)) * fg_ref[...]
        return out

    def mix(s, x1):
        h = _norm_mod(x1, p_gm, p_shift)
        y = _pool_mix(h, ptail_ref[...], pw_ref, head_seen, first=(s == 0))
        ptail_ref[...] = h[ts - POOL_HALO:]
        return x1 + y * p_scale

    if not pool:
        for s in range(n_sub):
            hidden(s)
            o_ref[0, pl.ds(s * ts, ts), :] = project(s)
    else:
        hidden(0)
        x1 = project(0)
        for s in range(1, n_sub):
            hidden(s)
            o_ref[0, pl.ds((s - 1) * ts, ts), :] = mix(s - 1, x1)
            x1 = project(s)
        o_ref[0, pl.ds((n_sub - 1) * ts, ts), :] = mix(n_sub - 1, x1)


def _cast_job(src, lead, n_steps, nj):
    r, c = src.shape[-2:]
    rows = next(k for k in range(BF16_SUBLANES, r + 1, BF16_SUBLANES)
                if r % k == 0 and r // k <= n_steps)
    last = r // rows - 1

    def block(b, j):
        return jnp.minimum(b * nj + j, last)

    src_spec = pl.BlockSpec((None,) * len(lead) + (rows, c),
                            lambda b, j: (*lead, block(b, j), 0))
    dst_spec = pl.BlockSpec((rows, c), lambda b, j: (block(b, j), 0))
    return src_spec, dst_spec, jax.ShapeDtypeStruct((r, c), BF16)


def _ffn(x, mod, norm_g, w_in, w_out, *, layer, sub, tm, ts, final_g=None, pool=None,
         cast=()):
    bsz, t, d = x.shape
    nj = t // tm
    final = final_g is not None
    kern = functools.partial(_ffn_kernel, sub=sub, g_row=N_SUB * layer + sub, final=final,
                             pool=pool is not None, ts=ts, n_cast=len(cast))
    resident = pl.Buffered(1)
    x_spec = pl.BlockSpec((1, tm, d), lambda b, j: (b, j, 0))
    in_specs = [
        x_spec,
        pl.BlockSpec((1, 1, N_SUB * N_MOD, d), lambda b, j: (layer, b, 0, 0)),
        pl.BlockSpec(norm_g.shape, lambda b, j: (0, 0)),
        pl.BlockSpec(w_in.shape, lambda b, j: (0, 0), pipeline_mode=resident),
        pl.BlockSpec(w_out.shape, lambda b, j: (0, 0), pipeline_mode=resident),
    ]
    args = [x, mod, norm_g, w_in, w_out]
    if final:
        in_specs.append(pl.BlockSpec((1, d), lambda b, j: (0, 0)))
        args.append(final_g.reshape(1, d))
    scratch = [pltpu.VMEM((2, ts, D_FF), BF16)]
    vmem = [4 * _nbytes((tm, d), F32), _nbytes(w_in.shape, BF16), _nbytes(w_out.shape, BF16),
            _nbytes((2, ts, D_FF), BF16)]
    if pool is not None:
        pool_w, pool_scale, j_pool = pool
        in_specs += [pl.BlockSpec(pool_w.shape, lambda b, j: (0, 0, 0)),
                     pl.BlockSpec((1, d), lambda b, j: (j_pool, 0))]
        args += [pool_w, pool_scale]
        scratch.append(pltpu.VMEM((POOL_HALO, d), F32))
        vmem.append(2 * _nbytes(pool_w.shape, BF16))
    out_specs = [x_spec]
    out_shape = [jax.ShapeDtypeStruct(x.shape, x.dtype)]
    for src, lead in cast:
        src_spec, dst_spec, dst_shape = _cast_job(src, lead, bsz * nj, nj)
        in_specs.append(src_spec)
        args.append(src)
        out_specs.append(dst_spec)
        out_shape.append(dst_shape)
        vmem += [2 * _nbytes(dst_spec.block_shape, F32), 2 * _nbytes(dst_spec.block_shape, BF16)]
    outs = pl.pallas_call(
        kern,
        grid=(bsz, nj),
        in_specs=in_specs,
        out_specs=out_specs,
        out_shape=out_shape,
        scratch_shapes=scratch,
        compiler_params=pltpu.CompilerParams(
            dimension_semantics=("arbitrary", "arbitrary"), vmem_limit_bytes=_vmem_limit(*vmem)),
        name=f"ffn_l{layer}_s{sub}",
    )(*args)
    return outs[0], outs[1:]


def _conv_kernel(x_ref, mod_ref, g_ref, win_ref, cw_ref, wout_ref, o_ref,
                 tail_ref, hb_ref, conv_ref, z_ref, *, sub, g_row, ts):
    j = pl.program_id(1)
    d = D_MODEL
    n_sub = x_ref.shape[1] // ts
    n_col = d // CONV_CHUNK
    gm, shift, gate = _mod_rows(mod_ref, g_ref, g_row, sub)

    @pl.when(j == 0)
    def _():
        tail_ref[...] = jnp.zeros_like(tail_ref)

    def stage_a(s):
        hb = _norm_mod(x_ref[0, pl.ds(s * ts, ts), :], gm, shift).astype(BF16)
        hb_ref[s % 2] = hb
        for n in range(n_col):
            lo = n * CONV_CHUNK
            cg = jnp.dot(hb, win_ref[:, d + lo:d + lo + CONV_CHUNK],
                         preferred_element_type=F32)
            v = jnp.dot(hb, win_ref[:, 2 * d + lo:2 * d + lo + CONV_CHUNK],
                        preferred_element_type=F32)
            u = cg * v
            ext = jnp.concatenate([tail_ref[:, lo:lo + CONV_CHUNK], u], axis=0)
            conv_ref[s % 2, :, lo:lo + CONV_CHUNK] = (
                cw_ref[0, 2:3, lo:lo + CONV_CHUNK] * u
                + cw_ref[0, 1:2, lo:lo + CONV_CHUNK] * pltpu.roll(ext, 1, 0)[CONV_HALO:]
                + cw_ref[0, 0:1, lo:lo + CONV_CHUNK] * pltpu.roll(ext, 2, 0)[CONV_HALO:])
            tail_ref[:, lo:lo + CONV_CHUNK] = u[ts - CONV_HALO:]

    def stage_b(s):
        hb = hb_ref[s % 2]
        for n in range(n_col):
            lo = n * CONV_CHUNK
            bg = jnp.dot(hb, win_ref[:, lo:lo + CONV_CHUNK], preferred_element_type=F32)
            z_ref[s % 2, :, lo:lo + CONV_CHUNK] = (
                bg * conv_ref[s % 2, :, lo:lo + CONV_CHUNK]).astype(BF16)

    def stage_c(s):
        rows = pl.ds(s * ts, ts)
        y = jnp.dot(z_ref[s % 2], wout_ref[...], preferred_element_type=F32)
        o_ref[0, rows, :] = x_ref[0, rows, :] + gate * y

    stage_a(0)
    stage_b(0)
    for s in range(1, n_sub):
        stage_a(s)
        stage_c(s - 1)
        stage_b(s)
    stage_c(n_sub - 1)


def _conv(x, mod, norm_g, w_in, w_conv, w_out, *, layer, j_conv, sub, tm, ts):
    bsz, t, d = x.shape
    kern = functools.partial(_conv_kernel, sub=sub, g_row=N_SUB * layer + sub, ts=ts)
    resident = pl.Buffered(1)
    return pl.pallas_call(
        kern,
        grid=(bsz, t // tm),
        in_specs=[
            pl.BlockSpec((1, tm, d), lambda b, j: (b, j, 0)),
            pl.BlockSpec((1, 1, N_SUB * N_MOD, d), lambda b, j: (layer, b, 0, 0)),
            pl.BlockSpec(norm_g.shape, lambda b, j: (0, 0)),
            pl.BlockSpec(w_in.shape, lambda b, j: (0, 0), pipeline_mode=resident),
            pl.BlockSpec((1, CONV_WIDTH, d), lambda b, j: (j_conv, 0, 0)),
            pl.BlockSpec(w_out.shape, lambda b, j: (0, 0), pipeline_mode=resident),
        ],
        out_specs=pl.BlockSpec((1, tm, d), lambda b, j: (b, j, 0)),
        out_shape=jax.ShapeDtypeStruct(x.shape, x.dtype),
        scratch_shapes=[pltpu.VMEM((CONV_HALO, d), F32), pltpu.VMEM((2, ts, d), BF16),
                        pltpu.VMEM((2, ts, d), F32), pltpu.VMEM((2, ts, d), BF16)],
        compiler_params=pltpu.CompilerParams(
            dimension_semantics=("arbitrary", "arbitrary"),
            vmem_limit_bytes=_vmem_limit(
                4 * _nbytes((tm, d), F32), _nbytes(w_in.shape, BF16), _nbytes(w_out.shape, BF16),
                _nbytes((CONV_HALO, d), F32), 2 * _nbytes((2, ts, d), BF16),
                _nbytes((2, ts, d), F32))),
        name=f"conv_l{layer}",
    )(x, mod, norm_g, w_in, w_conv, w_out)


def kernel(x, c, ada_w, ada_b, norm_g, ffn_w_in, ffn_w_out, pool_w, pool_scale,
           conv_w_in, conv_w, conv_w_out, final_g):
    bsz, t, d = x.shape
    tm, ts = 1024, 256
    mod = _adaln(c, ada_w, ada_b).reshape(DEPTH, bsz, N_SUB * N_MOD, d)
    norm_g2 = norm_g.reshape(DEPTH * N_SUB, d)
    w_in, w_out = ffn_w_in[0, 0].astype(BF16), ffn_w_out[0, 0].astype(BF16)
    pool_w_b = pool_w.astype(BF16)
    cw_in = cw_out = None
    for i in range(DEPTH):
        jm = i // N_MIXERS
        cast = [(ffn_w_in, (i, 1)), (ffn_w_out, (i, 1))]
        if i % N_MIXERS == 0:
            x, (w_in, w_out) = _ffn(x, mod, norm_g2, w_in, w_out, layer=i, sub=0, tm=tm, ts=ts,
                                    pool=(pool_w_b[jm], pool_scale, jm), cast=cast)
        else:
            x, (w_in, w_out) = _ffn(x, mod, norm_g2, w_in, w_out, layer=i, sub=0, tm=tm, ts=ts,
                                    cast=cast)
            x = _conv(x, mod, norm_g2, cw_in, conv_w, cw_out,
                      layer=i, j_conv=jm, sub=1, tm=tm, ts=ts)
        last = i == DEPTH - 1
        cast = [] if last else [(ffn_w_in, (i + 1, 0)), (ffn_w_out, (i + 1, 0))]
        if not last and (i + 1) % N_MIXERS != 0:
            jn = (i + 1) // N_MIXERS
            cast += [(conv_w_in, (jn,)), (conv_w_out, (jn,))]
        x, nxt = _ffn(x, mod, norm_g2, w_in, w_out, layer=i, sub=2, tm=tm, ts=ts,
                      final_g=final_g if last else None, cast=cast)
        if not last:
            w_in, w_out, *conv_b = nxt
            if conv_b:
                cw_in, cw_out = conv_b
    return x
```
